```python
import math
import jax, jax.numpy as jnp
from jax import lax
import numpy as np

D_MODEL = 2048
BATCH = 8
SEQ = 2048
DEPTH = 1

HEAD_DIM = 128
N_HEADS = D_MODEL // HEAD_DIM
N_HEADS_SB = N_HEADS // 2
N_HEADS_DIL = N_HEADS - N_HEADS_SB
D_SB = N_HEADS_SB * HEAD_DIM
D_DIL = N_HEADS_DIL * HEAD_DIM
QKV_WIDTH = 3 * (D_SB + D_DIL)
DILATED_BRANCHES = ((128, 1), (512, 4), (2048, 16))
QUERY_BLOCK = 128
D_FF = 5504
CONV_WIDTH = 3
ROPE_THETA = 10000.0
RMS_EPS = 1e-6

kernel_name = 'hybrid_stickbreaking_dilated_convffn_layer'


def rmsnorm(x, gain):
    xf = x.astype(jnp.float32)
    y = xf * lax.rsqrt(jnp.mean(xf * xf, axis=-1, keepdims=True) + RMS_EPS)
    return (y * gain.astype(jnp.float32)).astype(x.dtype)


def head_rmsnorm(o, gain):
    H, Dh = o.shape[1], o.shape[3]
    of = o.astype(jnp.float32)
    y = of * lax.rsqrt(jnp.mean(of * of, axis=-1, keepdims=True) + RMS_EPS)
    return (y * gain.astype(jnp.float32).reshape(1, H, 1, Dh)).astype(o.dtype)


def apply_rope(x):
    S, Dh = x.shape[2], x.shape[3]
    inv_freq = ROPE_THETA ** (-jnp.arange(0, Dh, 2, dtype=jnp.float32) / Dh)
    ang = jnp.arange(S, dtype=jnp.float32)[:, None] * inv_freq[None, :]
    cos, sin = jnp.cos(ang), jnp.sin(ang)
    x1, x2 = jnp.split(x.astype(jnp.float32), 2, axis=-1)
    out = jnp.concatenate([x1 * cos - x2 * sin, x2 * cos + x1 * sin], axis=-1)
    return out.astype(x.dtype)


def to_heads(t, n_heads):
    B, S, _ = t.shape
    return t.reshape(B, S, n_heads, HEAD_DIM).transpose(0, 2, 1, 3)


def stick_breaking_attention(q, k, v):
    S, Dh = q.shape[2], q.shape[3]
    scale = Dh ** -0.5
    outs = []
    for blk in range(S // QUERY_BLOCK):
        q0 = blk * QUERY_BLOCK
        n_keys = q0 + QUERY_BLOCK
        q_blk = q[:, :, q0:n_keys]
        k_pre, v_pre = k[:, :, :n_keys], v[:, :, :n_keys]
        z = jnp.einsum('bhqd,bhkd->bhqk', q_blk, k_pre).astype(jnp.float32) * scale
        q_pos = q0 + jnp.arange(QUERY_BLOCK)
        k_pos = jnp.arange(n_keys)
        causal = k_pos[None, :] < q_pos[:, None]
        log_beta = jax.nn.log_sigmoid(z)
        log_keep = jnp.where(causal, jax.nn.log_sigmoid(-z), 0.0)
        log_remain = lax.cumsum(log_keep, axis=3, reverse=True) - log_keep
        a = jnp.where(causal, jnp.exp(log_beta + log_remain), 0.0)
        outs.append(jnp.einsum('bhqk,bhkd->bhqd', a.astype(v.dtype), v_pre))
    return jnp.concatenate(outs, axis=2)


def dilated_branch(q, k, v, window, dilation):
    B, H, S, Dh = q.shape
    n_back = window // dilation
    QB = QUERY_BLOCK
    L = S // dilation
    n_blocks = -(-L // QB)
    Lp = n_blocks * QB
    scale = Dh ** -0.5

    def to_sub(t):
        return t.reshape(B, H, L, dilation, Dh).transpose(0, 1, 3, 2, 4)

    qs = jnp.pad(to_sub(q), ((0, 0), (0, 0), (0, 0), (0, Lp - L), (0, 0)))
    pad_kv = ((0, 0), (0, 0), (0, 0), (QB, Lp - L), (0, 0))
    ks = jnp.pad(to_sub(k), pad_kv)
    vs = jnp.pad(to_sub(v), pad_kv)
    qb = qs.reshape(B, H, dilation, n_blocks, QB, Dh)

    def band(t):
        prev = t[:, :, :, :Lp].reshape(B, H, dilation, n_blocks, QB, Dh)
        cur = t[:, :, :, QB:QB + Lp].reshape(B, H, dilation, n_blocks, QB, Dh)
        return jnp.concatenate([prev, cur], axis=4)

    kb, vb = band(ks), band(vs)
    s = jnp.einsum('bhrnqd,bhrnkd->bhrnqk', qb, kb).astype(jnp.float32) * scale
    q_idx = jnp.arange(n_blocks)[:, None] * QB + jnp.arange(QB)[None, :]
    k_idx = jnp.arange(n_blocks)[:, None] * QB - QB + jnp.arange(2 * QB)[None, :]
    dist = q_idx[:, :, None] - k_idx[:, None, :]
    valid = (dist >= 0) & (dist <= n_back) & (k_idx[:, None, :] >= 0)
    s = jnp.where(valid, s, -jnp.inf)
    m = jnp.max(s, axis=-1, keepdims=True)
    p = jnp.exp(s - m)
    den = jnp.sum(p, axis=-1, keepdims=True)
    out = jnp.einsum('bhrnqk,bhrnkd->bhrnqd', p.astype(v.dtype), vb).astype(jnp.float32) / den
    lse = (m + jnp.log(den))[..., 0]
    out = out.reshape(B, H, dilation, Lp, Dh)[:, :, :, :L]
    out = out.transpose(0, 1, 3, 2, 4).reshape(B, H, S, Dh)
    lse = lse.reshape(B, H, dilation, Lp)[:, :, :, :L].transpose(0, 1, 3, 2).reshape(B, H, S)
    return out, lse


def dilated_attention(q, k, v):
    outs, lses = [], []
    for window, dilation in DILATED_BRANCHES:
        o, l = dilated_branch(q, k, v, window, dilation)
        outs.append(o)
        lses.append(l)
    w = jax.nn.softmax(jnp.stack(lses, axis=0), axis=0)
    out = jnp.sum(w[..., None] * jnp.stack(outs, axis=0), axis=0)
    return out.astype(q.dtype)


def conv_geglu_ffn(h, w_up, conv_w, conv_b, w_down):
    S = h.shape[1]
    u = jnp.einsum('bsd,df->bsf', h, w_up)
    up = jnp.pad(u, ((0, 0), (CONV_WIDTH - 1, 0), (0, 0)))
    u = sum(up[:, j:j + S] * conv_w[j] for j in range(CONV_WIDTH)) + conv_b
    gate, val = jnp.split(u, 2, axis=-1)
    y = jax.nn.gelu(gate, approximate=True) * val
    return jnp.einsum('bsf,fd->bsd', y, w_down)


def setup_inputs(seed: int = 0) -> dict:
    key = jax.random.key(seed)
    ks = jax.random.split(key, 16)
    f32 = jnp.float32

    def gain(k, n):
        return 1.0 + 0.05 * jax.random.normal(k, (DEPTH, n), f32)

    return {
        'x': jax.random.normal(ks[0], (BATCH, SEQ, D_MODEL), f32),
        'pre_mix_gain': gain(ks[1], D_MODEL),
        'post_mix_gain': gain(ks[2], D_MODEL),
        'pre_ffn_gain': gain(ks[3], D_MODEL),
        'post_ffn_gain': gain(ks[4], D_MODEL),
        'w_in': jax.random.normal(ks[5], (DEPTH, D_MODEL, QKV_WIDTH), f32) * D_MODEL ** -0.5,
        'sb_out_gain': gain(ks[6], D_SB),
        'dil_out_gain': gain(ks[7], D_DIL),
        'w_out': jax.random.normal(ks[8], (DEPTH, D_SB + D_DIL, D_MODEL), f32) * (D_SB + D_DIL) ** -0.5,
        'w_up': jax.random.normal(ks[9], (DEPTH, D_MODEL, 2 * D_FF), f32) * D_MODEL ** -0.5,
        'conv_w': jax.random.normal(ks[10], (DEPTH, CONV_WIDTH, 2 * D_FF), f32) * CONV_WIDTH ** -0.5,
        'conv_b': 0.02 * jax.random.normal(ks[11], (DEPTH, 2 * D_FF), f32),
        'w_down': jax.random.normal(ks[12], (DEPTH, D_FF, D_MODEL), f32) * D_FF ** -0.5,
    }


def reference(x, pre_mix_gain, post_mix_gain, pre_ffn_gain, post_ffn_gain, w_in,
              sb_out_gain, dil_out_gain, w_out, w_up, conv_w, conv_b, w_down):
    splits = [D_SB, 2 * D_SB, 3 * D_SB, 3 * D_SB + D_DIL, 3 * D_SB + 2 * D_DIL]
    for layer in range(DEPTH):
        h = rmsnorm(x, pre_mix_gain[layer])
        proj = jnp.einsum('bsd,de->bse', h, w_in[layer])
        q_sb, k_sb, v_sb, q_dl, k_dl, v_dl = jnp.split(proj, splits, axis=-1)
        o_sb = stick_breaking_attention(to_heads(q_sb, N_HEADS_SB), to_heads(k_sb, N_HEADS_SB),
                                        to_heads(v_sb, N_HEADS_SB))
        o_dl = dilated_attention(apply_rope(to_heads(q_dl, N_HEADS_DIL)),
                                 apply_rope(to_heads(k_dl, N_HEADS_DIL)),
                                 to_heads(v_dl, N_HEADS_DIL))
        o_sb = head_rmsnorm(o_sb, sb_out_gain[layer])
        o_dl = head_rmsnorm(o_dl, dil_out_gain[layer])
        B, _, S, _ = o_sb.shape
        mixed = jnp.concatenate([o_sb.transpose(0, 2, 1, 3).reshape(B, S, D_SB),
                                 o_dl.transpose(0, 2, 1, 3).reshape(B, S, D_DIL)], axis=-1)
        mix_out = jnp.einsum('bse,ed->bsd', mixed, w_out[layer])
        x = x + rmsnorm(mix_out, post_mix_gain[layer])
        h = rmsnorm(x, pre_ffn_gain[layer])
        f = conv_geglu_ffn(h, w_up[layer], conv_w[layer], conv_b[layer], w_down[layer])
        x = x + rmsnorm(f, post_ffn_gain[layer])
    return x
```

```python
import functools
import math

import jax
import jax.numpy as jnp
from jax import lax
from jax.experimental import pallas as pl
from jax.experimental.pallas import tpu as pltpu

F32 = jnp.float32
BF16 = jnp.bfloat16

D_MODEL = 2048
HEAD_DIM = 128
N_HEADS_SB = 8
N_HEADS_DIL = 8
D_SB = N_HEADS_SB * HEAD_DIM
D_DIL = N_HEADS_DIL * HEAD_DIM
DILATIONS = (1, 4, 16)
N_BACK = 128
D_FF = 5504
CONV_WIDTH = 3
ROPE_THETA = 10000.0
RMS_EPS = 1e-6
SCALE = HEAD_DIM ** -0.5

LANES = 128
VMEM_LIMIT = 56 * 1024 * 1024

PROJ_TM = 512
SECTION = 1024
SB_HEADS = 4
SB_TQ = 256
SB_TK = 128
DIL_HEADS = 2
DIL_QB = 128
FFN_TM = 512
FFN_FC = 512
FFN_HALO = 16
D_FF_PAD = -(-D_FF // FFN_FC) * FFN_FC


def _rms(x, gain):
    return x * lax.rsqrt(jnp.mean(x * x, axis=-1, keepdims=True) + RMS_EPS) * gain


def _qkv_kernel(x_ref, g_ref, w_ref, cos_ref, sin_ref, sb_ref, dl_ref, h_ref):
    j = pl.program_id(1)

    @pl.when(j == 0)
    def _():
        h_ref[...] = _rms(x_ref[...], g_ref[...]).astype(BF16)

    acc = jnp.dot(h_ref[...], w_ref[...], preferred_element_type=F32)

    def rope(a):
        cos, sin = cos_ref[...], sin_ref[...]
        parts = []
        for h in range(SECTION // HEAD_DIM):
            a_h = a[:, h * HEAD_DIM:(h + 1) * HEAD_DIM]
            parts.append(a_h * cos + pltpu.roll(a_h, HEAD_DIM // 2, axis=1) * sin)
        return jnp.concatenate(parts, axis=1)

    @pl.when(j == 0)
    def _():
        sb_ref[...] = (acc * SCALE).astype(BF16)

    @pl.when((j == 1) | (j == 2))
    def _():
        sb_ref[...] = acc.astype(BF16)

    @pl.when(j == 3)
    def _():
        dl_ref[...] = rope(acc) * SCALE

    @pl.when(j == 4)
    def _():
        dl_ref[...] = rope(acc)

    @pl.when(j == 5)
    def _():
        dl_ref[...] = acc


def _qkv_proj(x2, gain, w_in, cos_t, sin_t, seq):
    T = x2.shape[0]
    tm = PROJ_TM
    n_pos_blocks = seq // tm
    return pl.pallas_call(
        _qkv_kernel,
        grid=(T // tm, 6),
        in_specs=[
            pl.BlockSpec((tm, D_MODEL), lambda i, j: (i, 0)),
            pl.BlockSpec((1, D_MODEL), lambda i, j: (0, 0)),
            pl.BlockSpec((D_MODEL, SECTION), lambda i, j: (0, j)),
            pl.BlockSpec((tm, HEAD_DIM), lambda i, j: (i % n_pos_blocks, 0)),
            pl.BlockSpec((tm, HEAD_DIM), lambda i, j: (i % n_pos_blocks, 0)),
        ],
        out_specs=[
            pl.BlockSpec((tm, SECTION), lambda i, j: (i, jnp.minimum(j, 2))),
            pl.BlockSpec((tm, SECTION), lambda i, j: (i, jnp.maximum(j - 3, 0))),
        ],
        out_shape=[
            jax.ShapeDtypeStruct((T, 3 * D_SB), BF16),
            jax.ShapeDtypeStruct((T, 3 * D_DIL), F32),
        ],
        scratch_shapes=[pltpu.VMEM((tm, D_MODEL), BF16)],
        compiler_params=pltpu.CompilerParams(
            dimension_semantics=("arbitrary", "arbitrary"), vmem_limit_bytes=VMEM_LIMIT),
        name="qkv_proj",
    )(x2, gain, w_in, cos_t, sin_t)


def _sb_kernel(q_ref, k_ref, v_ref, g_ref, uu_ref, o_ref, c_ref, acc_ref):
    seq = q_ref.shape[0]
    tq, tk = SB_TQ, SB_TK
    kb_per_q = tq // tk
    rel = (lax.broadcasted_iota(jnp.int32, (tq, tk), 0)
           - lax.broadcasted_iota(jnp.int32, (tq, tk), 1))
    nt = (((1,), (1,)), ((), ()))

    def block(qb, kb, masked):
        q0 = pl.multiple_of(qb * tq, tq)
        k0 = pl.multiple_of(kb * tk, tk)
        if masked:
            causal = rel > (k0 - q0)
        for h in range(SB_HEADS):
            hs = slice(h * HEAD_DIM, (h + 1) * HEAD_DIM)
            q = q_ref[pl.ds(q0, tq), hs]
            k = k_ref[pl.ds(k0, tk), hs]
            v = v_ref[pl.ds(k0, tk), hs]
            z = lax.dot_general(q, k, nt, preferred_element_type=F32)
            log_beta = jnp.minimum(z, 0.0) - jnp.log(1.0 + jnp.exp(-jnp.abs(z)))
            log_keep = log_beta - z
            if masked:
                log_keep = jnp.where(causal, log_keep, 0.0)
            hi = log_keep.astype(BF16)
            lo = (log_keep - hi.astype(F32)).astype(BF16)
            r = jnp.dot(jnp.concatenate([hi, lo], axis=1), uu_ref[...], preferred_element_type=F32)
            a = jnp.exp(log_beta + r[:, :tk] + c_ref[h])
            if masked:
                a = jnp.where(causal, a, 0.0)
            acc_ref[h] += jnp.dot(a.astype(BF16), v, preferred_element_type=F32)
            c_ref[h] += r[:, tk:]

    def q_tile(qb, _):
        c_ref[...] = jnp.zeros_like(c_ref)
        acc_ref[...] = jnp.zeros_like(acc_ref)
        for d in range(kb_per_q - 1, -1, -1):
            block(qb, qb * kb_per_q + d, True)

        def off_diag(i, _):
            block(qb, qb * kb_per_q - 1 - i, False)
            return 0

        lax.fori_loop(0, qb * kb_per_q, off_diag, 0)
        q0 = pl.multiple_of(qb * tq, tq)
        for h in range(SB_HEADS):
            hs = slice(h * HEAD_DIM, (h + 1) * HEAD_DIM)
            o_ref[pl.ds(q0, tq), hs] = _rms(acc_ref[h], g_ref[:, hs]).astype(o_ref.dtype)
        return 0

    lax.fori_loop(0, seq // tq, q_tile, 0)


def _sb_attention(proj_sb, gain, batch, seq):
    width = SB_HEADS * HEAD_DIM
    n_groups = D_SB // width
    p3 = proj_sb.reshape(batch, seq, 3 * D_SB)
    tri = (jnp.arange(SB_TK)[:, None] > jnp.arange(SB_TK)[None, :])
    u = jnp.concatenate([tri.astype(BF16), jnp.ones((SB_TK, SB_TK), BF16)], axis=1)
    uu = jnp.concatenate([u, u], axis=0)
    out = pl.pallas_call(
        _sb_kernel,
        grid=(batch, n_groups),
        in_specs=[
            pl.BlockSpec((None, seq, width), lambda b, g: (b, 0, g)),
            pl.BlockSpec((None, seq, width), lambda b, g: (b, 0, n_groups + g)),
            pl.BlockSpec((None, seq, width), lambda b, g: (b, 0, 2 * n_groups + g)),
            pl.BlockSpec((1, width), lambda b, g: (0, g)),
            pl.BlockSpec((2 * SB_TK, 2 * SB_TK), lambda b, g: (0, 0)),
        ],
        out_specs=pl.BlockSpec((None, seq, width), lambda b, g: (b, 0, g)),
        out_shape=jax.ShapeDtypeStruct((batch, seq, D_SB), BF16),
        scratch_shapes=[pltpu.VMEM((SB_HEADS, SB_TQ, SB_TK), F32),
                        pltpu.VMEM((SB_HEADS, SB_TQ, HEAD_DIM), F32)],
        compiler_params=pltpu.CompilerParams(
            dimension_semantics=("arbitrary", "arbitrary"), vmem_limit_bytes=VMEM_LIMIT),
        name="sb_attn",
    )(p3, p3, p3, gain, uu)
    return out.reshape(batch * seq, D_SB)


def _dil_kernel(*refs):
    q_refs, k_refs, v_refs = (refs[i * DIL_HEADS:(i + 1) * DIL_HEADS] for i in range(3))
    g_ref, o_ref, acc_s, m_s, den_s = refs[3 * DIL_HEADS:]
    seq = o_ref.shape[0]
    qb = DIL_QB
    nt = (((1,), (1,)), ((), ()))
    i2 = lax.broadcasted_iota(jnp.int32, (qb, 2 * qb), 0)
    j2 = lax.broadcasted_iota(jnp.int32, (qb, 2 * qb), 1)
    band = (j2 >= i2) & (j2 <= i2 + N_BACK)
    i1 = lax.broadcasted_iota(jnp.int32, (qb, qb), 0)
    j1 = lax.broadcasted_iota(jnp.int32, (qb, qb), 1)
    tril = j1 <= i1

    def rows(start, d):
        return pl.ds(start, qb, stride=d) if d > 1 else pl.ds(pl.multiple_of(start, qb), qb)

    for h in range(DIL_HEADS):
        hs = slice(h * HEAD_DIM, (h + 1) * HEAD_DIM)
        q_ref, k_ref, v_ref = q_refs[h], k_refs[h], v_refs[h]
        for bi, d in enumerate(DILATIONS):
            n_blocks = seq // d // qb

            def sub_block(g, _, bi=bi, d=d, n_blocks=n_blocks, q_ref=q_ref, k_ref=k_ref, v_ref=v_ref):
                r = g // n_blocks
                n = g % n_blocks
                cur = r + n * (qb * d)
                q = q_ref[rows(cur, d), :].astype(BF16)
                if n_blocks > 1:
                    prev = r + jnp.maximum(n - 1, 0) * (qb * d)
                    k = jnp.concatenate([k_ref[rows(prev, d), :], k_ref[rows(cur, d), :]], axis=0)
                    v = jnp.concatenate([v_ref[rows(prev, d), :], v_ref[rows(cur, d), :]], axis=0)
                    first_key = jnp.where(n > 0, 0, qb)
                    valid = band & (j2 >= first_key)
                else:
                    k = k_ref[rows(cur, d), :]
                    v = v_ref[rows(cur, d), :]
                    valid = tril
                s = lax.dot_general(q, k.astype(BF16), nt, preferred_element_type=F32)
                s = jnp.where(valid, s, -jnp.inf)
                m = jnp.max(s, axis=1, keepdims=True)
                p = jnp.exp(s - m)
                den = jnp.sum(p, axis=1, keepdims=True)
                acc = jnp.dot(p.astype(BF16), v.astype(BF16), preferred_element_type=F32)
                acc_s[bi, rows(cur, d), :] = acc
                m_s[bi, rows(cur, d), :] = jnp.broadcast_to(m, (qb, HEAD_DIM))
                den_s[bi, rows(cur, d), :] = jnp.broadcast_to(den, (qb, HEAD_DIM))
                return 0

            lax.fori_loop(0, seq // qb, sub_block, 0)

        def merge(t, _, hs=hs):
            rs = pl.ds(pl.multiple_of(t * 256, 256), 256)
            m0, m1, m2 = m_s[0, rs, :], m_s[1, rs, :], m_s[2, rs, :]
            mx = jnp.maximum(jnp.maximum(m0, m1), m2)
            w0, w1, w2 = jnp.exp(m0 - mx), jnp.exp(m1 - mx), jnp.exp(m2 - mx)
            num = w0 * acc_s[0, rs, :] + w1 * acc_s[1, rs, :] + w2 * acc_s[2, rs, :]
            den = w0 * den_s[0, rs, :] + w1 * den_s[1, rs, :] + w2 * den_s[2, rs, :]
            o_ref[rs, hs] = _rms(num / den, g_ref[:, hs]).astype(o_ref.dtype)
            return 0

        lax.fori_loop(0, seq // 256, merge, 0)


def _dil_attention(proj_dl, gain, batch, seq):
    width = DIL_HEADS * HEAD_DIM
    n_groups = D_DIL // width
    p3 = proj_dl.reshape(batch, seq, 3 * D_DIL)
    out = pl.pallas_call(
        _dil_kernel,
        grid=(batch, n_groups),
        in_specs=[
            pl.BlockSpec((None, seq, HEAD_DIM),
                         functools.partial(lambda b, g, *, off: (b, 0, off + g * DIL_HEADS),
                                           off=part * N_HEADS_DIL + h))
            for part in range(3) for h in range(DIL_HEADS)
        ] + [pl.BlockSpec((1, width), lambda b, g: (0, g))],
        out_specs=pl.BlockSpec((None, seq, width), lambda b, g: (b, 0, g)),
        out_shape=jax.ShapeDtypeStruct((batch, seq, D_DIL), BF16),
        scratch_shapes=[pltpu.VMEM((len(DILATIONS), seq, HEAD_DIM), F32)] * 3,
        compiler_params=pltpu.CompilerParams(
            dimension_semantics=("arbitrary", "arbitrary"), vmem_limit_bytes=VMEM_LIMIT),
        name="dil_attn",
    )(*([p3] * (3 * DIL_HEADS)), gain)
    return out.reshape(batch * seq, D_DIL)


def _out_kernel(sb_ref, dl_ref, w_ref, x_ref, gpost_ref, gpre_ref, x1_ref, h2_ref):
    mix = jnp.dot(sb_ref[...], w_ref[:D_SB, :], preferred_element_type=F32)
    mix += jnp.dot(dl_ref[...], w_ref[D_SB:, :], preferred_element_type=F32)
    x1 = x_ref[...] + _rms(mix, gpost_ref[...])
    x1_ref[...] = x1
    h2_ref[...] = _rms(x1, gpre_ref[...]).astype(BF16)


def _out_proj(mixed_sb, mixed_dl, w_out, x2, g_post, g_pre_ffn):
    T = x2.shape[0]
    tm = PROJ_TM
    return pl.pallas_call(
        _out_kernel,
        grid=(T // tm,),
        in_specs=[
            pl.BlockSpec((tm, D_SB), lambda i: (i, 0)),
            pl.BlockSpec((tm, D_DIL), lambda i: (i, 0)),
            pl.BlockSpec((D_SB + D_DIL, D_MODEL), lambda i: (0, 0)),
            pl.BlockSpec((tm, D_MODEL), lambda i: (i, 0)),
            pl.BlockSpec((1, D_MODEL), lambda i: (0, 0)),
            pl.BlockSpec((1, D_MODEL), lambda i: (0, 0)),
        ],
        out_specs=[
            pl.BlockSpec((tm, D_MODEL), lambda i: (i, 0)),
            pl.BlockSpec((tm, D_MODEL), lambda i: (i, 0)),
        ],
        out_shape=[
            jax.ShapeDtypeStruct((T, D_MODEL), F32),
            jax.ShapeDtypeStruct((T, D_MODEL), BF16),
        ],
        compiler_params=pltpu.CompilerParams(
            dimension_semantics=("arbitrary",), vmem_limit_bytes=VMEM_LIMIT),
        name="out_proj",
    )(mixed_sb, mixed_dl, w_out, x2, g_post, g_pre_ffn)


def _ffn_kernel(h_ref, halo_ref, wup_ref, cw_ref, cb_ref, wdn_ref, x1_ref, g_ref, o_ref,
                hext_ref, acc_ref, *, tiles_per_seq):
    i = pl.program_id(0)
    f = pl.program_id(1)
    tm, fc, halo = FFN_TM, FFN_FC, FFN_HALO

    @pl.when(f == 0)
    def _():
        keep = (i % tiles_per_seq) != 0
        hext_ref[:halo, :] = jnp.where(keep, halo_ref[...], jnp.zeros_like(halo_ref))
        hext_ref[halo:, :] = h_ref[...]
        acc_ref[...] = jnp.zeros_like(acc_ref)

    u = jnp.dot(hext_ref[...], wup_ref[...], preferred_element_type=F32)
    cw = cw_ref[...]
    conv = cb_ref[...] + cw[CONV_WIDTH - 1:CONV_WIDTH, :] * u[halo:, :]
    for tap in range(CONV_WIDTH - 1):
        back = CONV_WIDTH - 1 - tap
        conv += cw[tap:tap + 1, :] * u[halo - back:halo - back + tm, :]
    y = jax.nn.gelu(conv[:, :fc], approximate=True) * conv[:, fc:]
    acc_ref[...] += jnp.dot(y.astype(BF16), wdn_ref[...], preferred_element_type=F32)

    @pl.when(f == pl.num_programs(1) - 1)
    def _():
        o_ref[...] = x1_ref[...] + _rms(acc_ref[...], g_ref[...])


def _conv_ffn(h2, w_up_r, conv_w_r, conv_b_r, w_down_p, x1, g_post, seq):
    T = h2.shape[0]
    tm, fc, halo = FFN_TM, FFN_FC, FFN_HALO
    n_chunks = D_FF_PAD // fc
    halo_per_tile = tm // halo
    return pl.pallas_call(
        functools.partial(_ffn_kernel, tiles_per_seq=seq // tm),
        grid=(T // tm, n_chunks),
        in_specs=[
            pl.BlockSpec((tm, D_MODEL), lambda i, f: (i, 0)),
            pl.BlockSpec((halo, D_MODEL), lambda i, f: (jnp.maximum(i * halo_per_tile - 1, 0), 0)),
            pl.BlockSpec((D_MODEL, 2 * fc), lambda i, f: (0, f)),
            pl.BlockSpec((CONV_WIDTH, 2 * fc), lambda i, f: (0, f)),
            pl.BlockSpec((1, 2 * fc), lambda i, f: (0, f)),
            pl.BlockSpec((fc, D_MODEL), lambda i, f: (f, 0)),
            pl.BlockSpec((tm, D_MODEL), lambda i, f: (i, 0)),
            pl.BlockSpec((1, D_MODEL), lambda i, f: (0, 0)),
        ],
        out_specs=pl.BlockSpec((tm, D_MODEL), lambda i, f: (i, 0)),
        out_shape=jax.ShapeDtypeStruct((T, D_MODEL), F32),
        scratch_shapes=[pltpu.VMEM((halo + tm, D_MODEL), BF16),
                        pltpu.VMEM((tm, D_MODEL), F32)],
        compiler_params=pltpu.CompilerParams(
            dimension_semantics=("arbitrary", "arbitrary"), vmem_limit_bytes=VMEM_LIMIT),
        name="conv_ffn",
    )(h2, h2, w_up_r, conv_w_r, conv_b_r, w_down_p, x1, g_post)


def _chunk_gate_value(a):
    lead = a.shape[:-1]
    gv = a.reshape(lead + (2, D_FF))
    gv = jnp.pad(gv, [(0, 0)] * len(lead) + [(0, 0), (0, D_FF_PAD - D_FF)])
    gv = gv.reshape(lead + (2, D_FF_PAD // FFN_FC, FFN_FC))
    gv = jnp.swapaxes(gv, -3, -2)
    return gv.reshape(lead + (2 * D_FF_PAD,))


def _rope_tables(seq):
    inv_freq = ROPE_THETA ** (-jnp.arange(0, HEAD_DIM, 2, dtype=F32) / HEAD_DIM)
    ang = jnp.arange(seq, dtype=F32)[:, None] * inv_freq[None, :]
    cos, sin = jnp.cos(ang), jnp.sin(ang)
    return jnp.concatenate([cos, cos], axis=-1), jnp.concatenate([-sin, sin], axis=-1)


def kernel(x, pre_mix_gain, post_mix_gain, pre_ffn_gain, post_ffn_gain, w_in, sb_out_gain, dil_out_gain,
           w_out, w_up, conv_w, conv_b, w_down):
    batch, seq, _ = x.shape
    depth = w_in.shape[0]
    cos_t, sin_t = _rope_tables(seq)
    x2 = x.reshape(batch * seq, D_MODEL)
    for layer in range(depth):
        row = lambda g: g[layer][None, :]
        proj_sb, proj_dl = _qkv_proj(x2, row(pre_mix_gain), w_in[layer].astype(BF16), cos_t, sin_t, seq)
        mixed_sb = _sb_attention(proj_sb, row(sb_out_gain), batch, seq)
        mixed_dl = _dil_attention(proj_dl, row(dil_out_gain), batch, seq)
        x1, h2 = _out_proj(mixed_sb, mixed_dl, w_out[layer].astype(BF16), x2,
                           row(post_mix_gain), row(pre_ffn_gain))
        w_up_r = _chunk_gate_value(w_up[layer]).astype(BF16)
        w_down_p = jnp.pad(w_down[layer], ((0, D_FF_PAD - D_FF), (0, 0))).astype(BF16)
        x2 = _conv_ffn(h2, w_up_r, _chunk_gate_value(conv_w[layer]), _chunk_gate_value(conv_b[layer])[None, :],
                       w_down_p, x1, row(post_ffn_gain), seq)
    return x2.reshape(batch, seq, D_MODEL)
```

```python
import functools
import math

import jax
import jax.numpy as jnp
from jax import lax
from jax.experimental import pallas as pl
from jax.experimental.pallas import tpu as pltpu

F32 = jnp.float32
BF16 = jnp.bfloat16

D_MODEL = 2048
HEAD_DIM = 128
N_HEADS_SB = 8
N_HEADS_DIL = 8
D_SB = N_HEADS_SB * HEAD_DIM
D_DIL = N_HEADS_DIL * HEAD_DIM
DILATIONS = (1, 4, 16)
N_BACK = 128
D_FF = 5504
CONV_WIDTH = 3
ROPE_THETA = 10000.0
RMS_EPS = 1e-6
SCALE = HEAD_DIM ** -0.5
LOG2_E = math.log2(math.e)

LANES = 128
VMEM_LIMIT = 56 * 1024 * 1024

PROJ_TM = 512
SECTION = 1024
SB_HEADS = 4
SB_TQ = 256
SB_TK = 128
DIL_HEADS = 2
DIL_QB = 128
DIL_UNROLL = 4
FFN_TM = 512
FFN_FC = 512
FFN_HALO = 16
D_FF_PAD = -(-D_FF // FFN_FC) * FFN_FC


def _rms(x, gain):
    return x * lax.rsqrt(jnp.mean(x * x, axis=-1, keepdims=True) + RMS_EPS) * gain


def _qkv_kernel(x_ref, g_ref, w_ref, cos_ref, sin_ref, sb_ref, dl_ref, h_ref):
    j = pl.program_id(1)

    @pl.when(j == 0)
    def _():
        h_ref[...] = _rms(x_ref[...], g_ref[...]).astype(BF16)

    acc = jnp.dot(h_ref[...], w_ref[...], preferred_element_type=F32)

    def rope(a):
        cos, sin = cos_ref[...], sin_ref[...]
        parts = []
        for h in range(SECTION // HEAD_DIM):
            a_h = a[:, h * HEAD_DIM:(h + 1) * HEAD_DIM]
            parts.append(a_h * cos + pltpu.roll(a_h, HEAD_DIM // 2, axis=1) * sin)
        return jnp.concatenate(parts, axis=1)

    @pl.when(j == 0)
    def _():
        sb_ref[...] = (acc * (SCALE * LOG2_E)).astype(BF16)

    @pl.when((j == 1) | (j == 2))
    def _():
        sb_ref[...] = acc.astype(BF16)

    @pl.when(j == 3)
    def _():
        dl_ref[...] = rope(acc) * SCALE

    @pl.when(j == 4)
    def _():
        dl_ref[...] = rope(acc)

    @pl.when(j == 5)
    def _():
        dl_ref[...] = acc


def _qkv_proj(x2, gain, w_in, cos_t, sin_t, seq):
    T = x2.shape[0]
    tm = PROJ_TM
    n_pos_blocks = seq // tm
    return pl.pallas_call(
        _qkv_kernel,
        grid=(T // tm, 6),
        in_specs=[
            pl.BlockSpec((tm, D_MODEL), lambda i, j: (i, 0)),
            pl.BlockSpec((1, D_MODEL), lambda i, j: (0, 0)),
            pl.BlockSpec((D_MODEL, SECTION), lambda i, j: (0, j)),
            pl.BlockSpec((tm, HEAD_DIM), lambda i, j: (i % n_pos_blocks, 0)),
            pl.BlockSpec((tm, HEAD_DIM), lambda i, j: (i % n_pos_blocks, 0)),
        ],
        out_specs=[
            pl.BlockSpec((tm, SECTION), lambda i, j: (i, jnp.minimum(j, 2))),
            pl.BlockSpec((tm, SECTION), lambda i, j: (i, jnp.maximum(j - 3, 0))),
        ],
        out_shape=[
            jax.ShapeDtypeStruct((T, 3 * D_SB), BF16),
            jax.ShapeDtypeStruct((T, 3 * D_DIL), F32),
        ],
        scratch_shapes=[pltpu.VMEM((tm, D_MODEL), BF16)],
        compiler_params=pltpu.CompilerParams(
            dimension_semantics=("arbitrary", "arbitrary"), vmem_limit_bytes=VMEM_LIMIT),
        name="qkv_proj",
    )(x2, gain, w_in, cos_t, sin_t)


def _sb_kernel(q_ref, k_ref, v_ref, g_ref, uu_ref, o_ref, c_ref, acc_ref):
    seq = q_ref.shape[0]
    tq, tk = SB_TQ, SB_TK
    n_sub = tq // tk
    causal = (lax.broadcasted_iota(jnp.int32, (tq, tq), 0)
              > lax.broadcasted_iota(jnp.int32, (tq, tq), 1))
    nt = (((1,), (1,)), ((), ()))
    heads = range(SB_HEADS)
    hs = [slice(h * HEAD_DIM, (h + 1) * HEAD_DIM) for h in heads]

    def key_tile(qb, kt, masked):
        q0 = pl.multiple_of(qb * tq, tq)
        k0 = pl.multiple_of(kt * tq, tq)
        z = [lax.dot_general(q_ref[pl.ds(q0, tq), hs[h]], k_ref[pl.ds(k0, tq), hs[h]], nt,
                             preferred_element_type=F32) for h in heads]
        log_beta = [jnp.minimum(z[h], 0.0) - jnp.log2(1.0 + jnp.exp2(-jnp.abs(z[h]))) for h in heads]
        log_keep = [log_beta[h] - z[h] for h in heads]
        if masked:
            log_keep = [jnp.where(causal, lk, 0.0) for lk in log_keep]
        hi = [lk.astype(BF16) for lk in log_keep]
        lo = [(log_keep[h] - hi[h].astype(F32)).astype(BF16) for h in heads]
        r = [[jnp.dot(jnp.concatenate([hi[h][:, j * tk:(j + 1) * tk], lo[h][:, j * tk:(j + 1) * tk]], axis=1),
                      uu_ref[...], preferred_element_type=F32) for j in range(n_sub)] for h in heads]
        pv = []
        for h in heads:
            c = c_ref[h]
            a = [None] * n_sub
            for j in range(n_sub - 1, -1, -1):
                a[j] = jnp.exp2(log_beta[h][:, j * tk:(j + 1) * tk] + r[h][j][:, :tk] + c)
                c = c + r[h][j][:, tk:]
            c_ref[h] = c
            a = jnp.concatenate(a, axis=1)
            if masked:
                a = jnp.where(causal, a, 0.0)
            pv.append(jnp.dot(a.astype(BF16), v_ref[pl.ds(k0, tq), hs[h]], preferred_element_type=F32))
        for h in heads:
            acc_ref[h] += pv[h]

    def q_tile(qb, _):
        c_ref[...] = jnp.zeros_like(c_ref)
        acc_ref[...] = jnp.zeros_like(acc_ref)
        key_tile(qb, qb, True)

        def off_diag(i, _):
            key_tile(qb, qb - 1 - i, False)
            return 0

        lax.fori_loop(0, qb, off_diag, 0)
        q0 = pl.multiple_of(qb * tq, tq)
        for h in range(SB_HEADS):
            hs = slice(h * HEAD_DIM, (h + 1) * HEAD_DIM)
            o_ref[pl.ds(q0, tq), hs] = _rms(acc_ref[h], g_ref[:, hs]).astype(o_ref.dtype)
        return 0

    lax.fori_loop(0, seq // tq, q_tile, 0)


def _sb_attention(proj_sb, gain, batch, seq):
    width = SB_HEADS * HEAD_DIM
    n_groups = D_SB // width
    p3 = proj_sb.reshape(batch, seq, 3 * D_SB)
    tri = (jnp.arange(SB_TK)[:, None] > jnp.arange(SB_TK)[None, :])
    u = jnp.concatenate([tri.astype(BF16), jnp.ones((SB_TK, SB_TK), BF16)], axis=1)
    uu = jnp.concatenate([u, u], axis=0)
    out = pl.pallas_call(
        _sb_kernel,
        grid=(batch, n_groups),
        in_specs=[
            pl.BlockSpec((None, seq, width), lambda b, g: (b, 0, g)),
            pl.BlockSpec((None, seq, width), lambda b, g: (b, 0, n_groups + g)),
            pl.BlockSpec((None, seq, width), lambda b, g: (b, 0, 2 * n_groups + g)),
            pl.BlockSpec((1, width), lambda b, g: (0, g)),
            pl.BlockSpec((2 * SB_TK, 2 * SB_TK), lambda b, g: (0, 0)),
        ],
        out_specs=pl.BlockSpec((None, seq, width), lambda b, g: (b, 0, g)),
        out_shape=jax.ShapeDtypeStruct((batch, seq, D_SB), BF16),
        scratch_shapes=[pltpu.VMEM((SB_HEADS, SB_TQ, SB_TK), F32),
                        pltpu.VMEM((SB_HEADS, SB_TQ, HEAD_DIM), F32)],
        compiler_params=pltpu.CompilerParams(
            dimension_semantics=("arbitrary", "arbitrary"), vmem_limit_bytes=VMEM_LIMIT),
        name="sb_attn",
    )(p3, p3, p3, gain, uu)
    return out.reshape(batch * seq, D_SB)


def _dil_kernel(*refs):
    q_refs, k_refs, v_refs = (refs[i * DIL_HEADS:(i + 1) * DIL_HEADS] for i in range(3))
    g_ref, o_ref, acc_s, m_s, den_s = refs[3 * DIL_HEADS:]
    seq = o_ref.shape[0]
    qb = DIL_QB
    nt = (((1,), (1,)), ((), ()))
    i2 = lax.broadcasted_iota(jnp.int32, (qb, 2 * qb), 0)
    j2 = lax.broadcasted_iota(jnp.int32, (qb, 2 * qb), 1)
    band = (j2 >= i2) & (j2 <= i2 + N_BACK)
    i1 = lax.broadcasted_iota(jnp.int32, (qb, qb), 0)
    j1 = lax.broadcasted_iota(jnp.int32, (qb, qb), 1)
    tril = j1 <= i1

    def rows(start, d):
        return pl.ds(start, qb, stride=d) if d > 1 else pl.ds(pl.multiple_of(start, qb), qb)

    n_br = len(DILATIONS)
    for bi, d in enumerate(DILATIONS):
        n_blocks = seq // d // qb

        def sub_blocks(i, _, bi=bi, d=d, n_blocks=n_blocks):
            chains = [(h, i * DIL_UNROLL + u) for h in range(DIL_HEADS) for u in range(DIL_UNROLL)]
            cur, q, k, v, valid = [], [], [], [], []
            for h, g in chains:
                r = g // n_blocks
                n = g % n_blocks
                c0 = r + n * (qb * d)
                cur.append(c0)
                q.append(q_refs[h][rows(c0, d), :].astype(BF16))
                if n_blocks > 1:
                    p0 = r + jnp.maximum(n - 1, 0) * (qb * d)
                    k.append(jnp.concatenate([k_refs[h][rows(p0, d), :], k_refs[h][rows(c0, d), :]], axis=0))
                    v.append(jnp.concatenate([v_refs[h][rows(p0, d), :], v_refs[h][rows(c0, d), :]], axis=0))
                    valid.append(band & (j2 >= jnp.where(n > 0, 0, qb)))
                else:
                    k.append(k_refs[h][rows(c0, d), :])
                    v.append(v_refs[h][rows(c0, d), :])
                    valid.append(tril)
            idx = range(len(chains))
            s = [lax.dot_general(q[c], k[c].astype(BF16), nt, preferred_element_type=F32) for c in idx]
            s = [jnp.where(valid[c], s[c], -jnp.inf) for c in idx]
            m = [jnp.max(s[c], axis=1, keepdims=True) for c in idx]
            p = [jnp.exp(s[c] - m[c]) for c in idx]
            den = [jnp.sum(p[c], axis=1, keepdims=True) for c in idx]
            acc = [jnp.dot(p[c].astype(BF16), v[c].astype(BF16), preferred_element_type=F32) for c in idx]
            for c, (h, _) in enumerate(chains):
                acc_s[h * n_br + bi, rows(cur[c], d), :] = acc[c]
                m_s[h * n_br + bi, rows(cur[c], d), :] = jnp.broadcast_to(m[c], (qb, HEAD_DIM))
                den_s[h * n_br + bi, rows(cur[c], d), :] = jnp.broadcast_to(den[c], (qb, HEAD_DIM))
            return 0

        lax.fori_loop(0, seq // qb // DIL_UNROLL, sub_blocks, 0)

    def merge(t, _):
        rs = pl.ds(pl.multiple_of(t * qb, qb), qb)
        for h in range(DIL_HEADS):
            hs = slice(h * HEAD_DIM, (h + 1) * HEAD_DIM)
            ms = [m_s[h * n_br + bi, rs, :] for bi in range(n_br)]
            mx = functools.reduce(jnp.maximum, ms)
            w = [jnp.exp(m_b - mx) for m_b in ms]
            num = sum(w[bi] * acc_s[h * n_br + bi, rs, :] for bi in range(n_br))
            den = sum(w[bi] * den_s[h * n_br + bi, rs, :] for bi in range(n_br))
            o_ref[rs, hs] = _rms(num / den, g_ref[:, hs]).astype(o_ref.dtype)
        return 0

    lax.fori_loop(0, seq // qb, merge, 0)


def _dil_attention(proj_dl, gain, batch, seq):
    width = DIL_HEADS * HEAD_DIM
    n_groups = D_DIL // width
    p3 = proj_dl.reshape(batch, seq, 3 * D_DIL)
    out = pl.pallas_call(
        _dil_kernel,
        grid=(batch, n_groups),
        in_specs=[
            pl.BlockSpec((None, seq, HEAD_DIM),
                         functools.partial(lambda b, g, *, off: (b, 0, off + g * DIL_HEADS),
                                           off=part * N_HEADS_DIL + h))
            for part in range(3) for h in range(DIL_HEADS)
        ] + [pl.BlockSpec((1, width), lambda b, g: (0, g))],
        out_specs=pl.BlockSpec((None, seq, width), lambda b, g: (b, 0, g)),
        out_shape=jax.ShapeDtypeStruct((batch, seq, D_DIL), BF16),
        scratch_shapes=[pltpu.VMEM((DIL_HEADS * len(DILATIONS), seq, HEAD_DIM), F32)] * 3,
        compiler_params=pltpu.CompilerParams(
            dimension_semantics=("arbitrary", "arbitrary"), vmem_limit_bytes=VMEM_LIMIT),
        name="dil_attn",
    )(*([p3] * (3 * DIL_HEADS)), gain)
    return out.reshape(batch * seq, D_DIL)


def _out_kernel(sb_ref, dl_ref, w_ref, x_ref, gpost_ref, gpre_ref, x1_ref, h2_ref):
    mix = jnp.dot(sb_ref[...], w_ref[:D_SB, :], preferred_element_type=F32)
    mix += jnp.dot(dl_ref[...], w_ref[D_SB:, :], preferred_element_type=F32)
    x1 = x_ref[...] + _rms(mix, gpost_ref[...])
    x1_ref[...] = x1
    h2_ref[...] = _rms(x1, gpre_ref[...]).astype(BF16)


def _out_proj(mixed_sb, mixed_dl, w_out, x2, g_post, g_pre_ffn):
    T = x2.shape[0]
    tm = PROJ_TM
    return pl.pallas_call(
        _out_kernel,
        grid=(T // tm,),
        in_specs=[
            pl.BlockSpec((tm, D_SB), lambda i: (i, 0)),
            pl.BlockSpec((tm, D_DIL), lambda i: (i, 0)),
            pl.BlockSpec((D_SB + D_DIL, D_MODEL), lambda i: (0, 0)),
            pl.BlockSpec((tm, D_MODEL), lambda i: (i, 0)),
            pl.BlockSpec((1, D_MODEL), lambda i: (0, 0)),
            pl.BlockSpec((1, D_MODEL), lambda i: (0, 0)),
        ],
        out_specs=[
            pl.BlockSpec((tm, D_MODEL), lambda i: (i, 0)),
            pl.BlockSpec((tm, D_MODEL), lambda i: (i, 0)),
        ],
        out_shape=[
            jax.ShapeDtypeStruct((T, D_MODEL), F32),
            jax.ShapeDtypeStruct((T, D_MODEL), BF16),
        ],
        compiler_params=pltpu.CompilerParams(
            dimension_semantics=("arbitrary",), vmem_limit_bytes=VMEM_LIMIT),
        name="out_proj",
    )(mixed_sb, mixed_dl, w_out, x2, g_post, g_pre_ffn)


def _ffn_kernel(h_ref, halo_ref, wg_ref, wv_ref, cwg_ref, cwv_ref, cbg_ref, cbv_ref, wdn_ref, x1_ref, g_ref,
                o_ref, hext_ref, acc_ref, *, tiles_per_seq):
    i = pl.program_id(0)
    f = pl.program_id(1)
    tm, fc, halo = FFN_TM, FFN_FC, FFN_HALO

    @pl.when(f == 0)
    def _():
        keep = (i % tiles_per_seq) != 0
        hext_ref[:halo, :] = jnp.where(keep, halo_ref[...], jnp.zeros_like(halo_ref))
        hext_ref[halo:, :] = h_ref[...]
        acc_ref[...] = jnp.zeros_like(acc_ref)

    def up_conv(w_ref, cw_ref, cb_ref):
        u = jnp.dot(hext_ref[...], w_ref[...], preferred_element_type=F32)
        cw = cw_ref[...]
        conv = cb_ref[...] + cw[CONV_WIDTH - 1:CONV_WIDTH, :] * u[halo:, :]
        for tap in range(CONV_WIDTH - 1):
            back = CONV_WIDTH - 1 - tap
            conv += cw[tap:tap + 1, :] * u[halo - back:halo - back + tm, :]
        return conv

    gate = up_conv(wg_ref, cwg_ref, cbg_ref)
    value = up_conv(wv_ref, cwv_ref, cbv_ref)
    y = jax.nn.gelu(gate, approximate=True) * value
    acc_ref[...] += jnp.dot(y.astype(BF16), wdn_ref[...], preferred_element_type=F32)

    @pl.when(f == pl.num_programs(1) - 1)
    def _():
        o_ref[...] = x1_ref[...] + _rms(acc_ref[...], g_ref[...])


def _conv_ffn(h2, w_up_p, conv_w_p, conv_b_p, w_down_p, x1, g_post, seq):
    T = h2.shape[0]
    tm, fc, halo = FFN_TM, FFN_FC, FFN_HALO
    n_chunks = D_FF_PAD // fc
    halo_per_tile = tm // halo
    gate_blk = lambda i, f: (0, f)
    value_blk = lambda i, f: (0, n_chunks + f)
    return pl.pallas_call(
        functools.partial(_ffn_kernel, tiles_per_seq=seq // tm),
        grid=(T // tm, n_chunks),
        in_specs=[
            pl.BlockSpec((tm, D_MODEL), lambda i, f: (i, 0)),
            pl.BlockSpec((halo, D_MODEL), lambda i, f: (jnp.maximum(i * halo_per_tile - 1, 0), 0)),
            pl.BlockSpec((D_MODEL, fc), gate_blk),
            pl.BlockSpec((D_MODEL, fc), value_blk),
            pl.BlockSpec((CONV_WIDTH, fc), gate_blk),
            pl.BlockSpec((CONV_WIDTH, fc), value_blk),
            pl.BlockSpec((1, fc), gate_blk),
            pl.BlockSpec((1, fc), value_blk),
            pl.BlockSpec((fc, D_MODEL), lambda i, f: (f, 0)),
            pl.BlockSpec((tm, D_MODEL), lambda i, f: (i, 0)),
            pl.BlockSpec((1, D_MODEL), lambda i, f: (0, 0)),
        ],
        out_specs=pl.BlockSpec((tm, D_MODEL), lambda i, f: (i, 0)),
        out_shape=jax.ShapeDtypeStruct((T, D_MODEL), F32),
        scratch_shapes=[pltpu.VMEM((halo + tm, D_MODEL), BF16),
                        pltpu.VMEM((tm, D_MODEL), F32)],
        compiler_params=pltpu.CompilerParams(
            dimension_semantics=("arbitrary", "arbitrary"), vmem_limit_bytes=VMEM_LIMIT),
        name="conv_ffn",
    )(h2, h2, w_up_p, w_up_p, conv_w_p, conv_w_p, conv_b_p, conv_b_p, w_down_p, x1, g_post)


def _pad_gate_value(a, dtype):
    pad = ((0, 0), (0, D_FF_PAD - D_FF))
    return jnp.concatenate([jnp.pad(a[:, :D_FF].astype(dtype), pad), jnp.pad(a[:, D_FF:].astype(dtype), pad)], axis=1)


def _rope_tables(seq):
    inv_freq = ROPE_THETA ** (-jnp.arange(0, HEAD_DIM, 2, dtype=F32) / HEAD_DIM)
    ang = jnp.arange(seq, dtype=F32)[:, None] * inv_freq[None, :]
    cos, sin = jnp.cos(ang), jnp.sin(ang)
    return jnp.concatenate([cos, cos], axis=-1), jnp.concatenate([-sin, sin], axis=-1)


def kernel(x, pre_mix_gain, post_mix_gain, pre_ffn_gain, post_ffn_gain, w_in, sb_out_gain, dil_out_gain,
           w_out, w_up, conv_w, conv_b, w_down):
    batch, seq, _ = x.shape
    depth = w_in.shape[0]
    cos_t, sin_t = _rope_tables(seq)
    x2 = x.reshape(batch * seq, D_MODEL)
    for layer in range(depth):
        row = lambda g: g[layer][None, :]
        proj_sb, proj_dl = _qkv_proj(x2, row(pre_mix_gain), w_in[layer].astype(BF16), cos_t, sin_t, seq)
        mixed_sb = _sb_attention(proj_sb, row(sb_out_gain), batch, seq)
        mixed_dl = _dil_attention(proj_dl, row(dil_out_gain), batch, seq)
        x1, h2 = _out_proj(mixed_sb, mixed_dl, w_out[layer].astype(BF16), x2,
                           row(post_mix_gain), row(pre_ffn_gain))
        w_down_p = jnp.pad(w_down[layer].astype(BF16), ((0, D_FF_PAD - D_FF), (0, 0)))
        x2 = _conv_ffn(h2, _pad_gate_value(w_up[layer], BF16), _pad_gate_value(conv_w[layer], F32),
                       _pad_gate_value(row(conv_b), F32), w_down_p, x1, row(post_ffn_gain), seq)
    return x2.reshape(batch, seq, D_MODEL)
```

```python
import functools
import math

import jax
import jax.numpy as jnp
from jax import lax
from jax.experimental import pallas as pl
from jax.experimental.pallas import tpu as pltpu

F32 = jnp.float32
BF16 = jnp.bfloat16

D_MODEL = 2048
HEAD_DIM = 128
N_HEADS_SB = 8
N_HEADS_DIL = 8
D_SB = N_HEADS_SB * HEAD_DIM
D_DIL = N_HEADS_DIL * HEAD_DIM
DILATIONS = (1, 4, 16)
N_BACK = 128
D_FF = 5504
CONV_WIDTH = 3
ROPE_THETA = 10000.0
RMS_EPS = 1e-6
SCALE = HEAD_DIM ** -0.5
LOG2_E = math.log2(math.e)

LANES = 128
VMEM_LIMIT = 56 * 1024 * 1024

PROJ_TM = 1024
OUT_TM = 512
SECTION = 1024
SB_HEADS = 4
SB_TQ = 256
SB_TK = 128
SB_DEAD_LOG2 = -152.0
DIL_HEADS = 2
DIL_QB = 128
DIL_UNROLL = 4
FFN_TM = 1024
FFN_FC = 512
FFN_HALO = 8
D_FF_PAD = -(-D_FF // FFN_FC) * FFN_FC


def _rms(x, gain):
    return x * lax.rsqrt(jnp.mean(x * x, axis=-1, keepdims=True) + RMS_EPS) * gain


def _qkv_kernel(x_ref, g_ref, w_ref, ta_ref, tb_ref, o_ref, h_ref):
    j = pl.program_id(1)

    @pl.when(j == 0)
    def _():
        h_ref[...] = _rms(x_ref[...], g_ref[...]).astype(BF16)

    acc = jnp.dot(h_ref[...], w_ref[...], preferred_element_type=F32)
    ta, tb = ta_ref[...], tb_ref[...]
    for h in range(SECTION // HEAD_DIM):
        hs = slice(h * HEAD_DIM, (h + 1) * HEAD_DIM)
        a_h = acc[:, hs]
        o_ref[:, hs] = (a_h * ta + pltpu.roll(a_h, HEAD_DIM // 2, axis=1) * tb).astype(BF16)


def _section_tables(seq):
    inv_freq = ROPE_THETA ** (-jnp.arange(0, HEAD_DIM, 2, dtype=F32) / HEAD_DIM)
    ang = jnp.arange(seq, dtype=F32)[:, None] * inv_freq[None, :]
    cos, sin = jnp.cos(ang), jnp.sin(ang)
    cos_t = jnp.concatenate([cos, cos], axis=-1)
    sin_t = jnp.concatenate([-sin, sin], axis=-1)
    one, zero = jnp.ones_like(cos_t), jnp.zeros_like(cos_t)
    ta = jnp.stack([cos_t * SCALE, cos_t, one * (SCALE * LOG2_E), one])
    tb = jnp.stack([sin_t * SCALE, sin_t, zero, zero])
    return ta, tb


def _qkv_proj(x2, gain, w_in, ta, tb, seq):
    T = x2.shape[0]
    tm = PROJ_TM
    n_pos_blocks = seq // tm

    def table_blk(i, j):
        kind = jnp.where(j == 3, 0, jnp.where(j == 4, 1, jnp.where(j == 0, 2, 3)))
        return (kind, i % n_pos_blocks, 0)

    return pl.pallas_call(
        _qkv_kernel,
        grid=(T // tm, 6),
        in_specs=[
            pl.BlockSpec((tm, D_MODEL), lambda i, j: (i, 0)),
            pl.BlockSpec((1, D_MODEL), lambda i, j: (0, 0)),
            pl.BlockSpec((D_MODEL, SECTION), lambda i, j: (0, j)),
            pl.BlockSpec((None, tm, HEAD_DIM), table_blk),
            pl.BlockSpec((None, tm, HEAD_DIM), table_blk),
        ],
        out_specs=pl.BlockSpec((tm, SECTION), lambda i, j: (i, j)),
        out_shape=jax.ShapeDtypeStruct((T, 3 * (D_SB + D_DIL)), BF16),
        scratch_shapes=[pltpu.VMEM((tm, D_MODEL), BF16)],
        compiler_params=pltpu.CompilerParams(
            dimension_semantics=("arbitrary", "arbitrary"), vmem_limit_bytes=VMEM_LIMIT),
        name="qkv_proj",
    )(x2, gain, w_in, ta, tb)


def _sb_kernel(q_ref, k_ref, v_ref, g_ref, uu_ref, o_ref, c_ref, acc_ref):
    seq = q_ref.shape[0]
    tq, tk = SB_TQ, SB_TK
    n_sub = tq // tk
    causal = (lax.broadcasted_iota(jnp.int32, (tq, tq), 0)
              > lax.broadcasted_iota(jnp.int32, (tq, tq), 1))
    nt = (((1,), (1,)), ((), ()))
    heads = range(SB_HEADS)
    hs = [slice(h * HEAD_DIM, (h + 1) * HEAD_DIM) for h in heads]

    def key_tile(qb, kt, masked):
        q0 = pl.multiple_of(qb * tq, tq)
        k0 = pl.multiple_of(kt * tq, tq)
        z = [lax.dot_general(q_ref[pl.ds(q0, tq), hs[h]], k_ref[pl.ds(k0, tq), hs[h]], nt,
                             preferred_element_type=F32) for h in heads]
        log_beta = [jnp.minimum(z[h], 0.0) - jnp.log2(1.0 + jnp.exp2(-jnp.abs(z[h]))) for h in heads]
        log_keep = [log_beta[h] - z[h] for h in heads]
        if masked:
            log_keep = [jnp.where(causal, lk, 0.0) for lk in log_keep]
        hi = [lk.astype(BF16) for lk in log_keep]
        lo = [(log_keep[h] - hi[h].astype(F32)).astype(BF16) for h in heads]
        r = [[jnp.dot(jnp.concatenate([hi[h][:, j * tk:(j + 1) * tk], lo[h][:, j * tk:(j + 1) * tk]], axis=1),
                      uu_ref[...], preferred_element_type=F32) for j in range(n_sub)] for h in heads]
        pv = []
        for h in heads:
            c = c_ref[h]
            a = [None] * n_sub
            for j in range(n_sub - 1, -1, -1):
                a[j] = jnp.exp2(log_beta[h][:, j * tk:(j + 1) * tk] + r[h][j][:, :tk] + c)
                c = c + r[h][j][:, tk:]
            c_ref[h] = c
            a = jnp.concatenate(a, axis=1)
            if masked:
                a = jnp.where(causal, a, 0.0)
            pv.append(jnp.dot(a.astype(BF16), v_ref[pl.ds(k0, tq), hs[h]], preferred_element_type=F32))
        for h in heads:
            acc_ref[h] += pv[h]

    def q_tile(qb, _):
        c_ref[...] = jnp.zeros_like(c_ref)
        acc_ref[...] = jnp.zeros_like(acc_ref)
        key_tile(qb, qb, True)

        def live(carry):
            i, c_max = carry
            return (i < qb) & (c_max > SB_DEAD_LOG2)

        def off_diag(carry):
            i, _ = carry
            key_tile(qb, qb - 1 - i, False)
            return i + 1, jnp.max(functools.reduce(jnp.maximum, [c_ref[h] for h in heads]))

        lax.while_loop(live, off_diag, (jnp.int32(0), jnp.float32(0.0)))
        q0 = pl.multiple_of(qb * tq, tq)
        for h in range(SB_HEADS):
            hs = slice(h * HEAD_DIM, (h + 1) * HEAD_DIM)
            o_ref[pl.ds(q0, tq), hs] = _rms(acc_ref[h], g_ref[:, hs]).astype(o_ref.dtype)
        return 0

    lax.fori_loop(0, seq // tq, q_tile, 0)


def _sb_attention(proj, gain, batch, seq):
    width = SB_HEADS * HEAD_DIM
    n_groups = D_SB // width
    p3 = proj.reshape(batch, seq, proj.shape[-1])
    tri = (jnp.arange(SB_TK)[:, None] > jnp.arange(SB_TK)[None, :])
    u = jnp.concatenate([tri.astype(BF16), jnp.ones((SB_TK, SB_TK), BF16)], axis=1)
    uu = jnp.concatenate([u, u], axis=0)
    out = pl.pallas_call(
        _sb_kernel,
        grid=(batch, n_groups),
        in_specs=[
            pl.BlockSpec((None, seq, width), lambda b, g: (b, 0, g)),
            pl.BlockSpec((None, seq, width), lambda b, g: (b, 0, n_groups + g)),
            pl.BlockSpec((None, seq, width), lambda b, g: (b, 0, 2 * n_groups + g)),
            pl.BlockSpec((1, width), lambda b, g: (0, g)),
            pl.BlockSpec((2 * SB_TK, 2 * SB_TK), lambda b, g: (0, 0)),
        ],
        out_specs=pl.BlockSpec((None, seq, width), lambda b, g: (b, 0, g)),
        out_shape=jax.ShapeDtypeStruct((batch, seq, D_SB), BF16),
        scratch_shapes=[pltpu.VMEM((SB_HEADS, SB_TQ, SB_TK), F32),
                        pltpu.VMEM((SB_HEADS, SB_TQ, HEAD_DIM), F32)],
        compiler_params=pltpu.CompilerParams(
            dimension_semantics=("arbitrary", "arbitrary"), vmem_limit_bytes=VMEM_LIMIT),
        name="sb_attn",
    )(p3, p3, p3, gain, uu)
    return out.reshape(batch * seq, D_SB)


def _dil_kernel(*refs):
    g_ref, o_ref, qkv_s, acc_s, m_s, den_s = refs[3 * DIL_HEADS:]
    seq = o_ref.shape[0]
    for i in range(3 * DIL_HEADS):
        qkv_s[i] = refs[i][...].astype(F32)
    q_refs, k_refs, v_refs = ([qkv_s.at[p * DIL_HEADS + h] for h in range(DIL_HEADS)] for p in range(3))
    qb = DIL_QB
    nt = (((1,), (1,)), ((), ()))
    i2 = lax.broadcasted_iota(jnp.int32, (qb, 2 * qb), 0)
    j2 = lax.broadcasted_iota(jnp.int32, (qb, 2 * qb), 1)
    band = (j2 >= i2) & (j2 <= i2 + N_BACK)
    i1 = lax.broadcasted_iota(jnp.int32, (qb, qb), 0)
    j1 = lax.broadcasted_iota(jnp.int32, (qb, qb), 1)
    tril = j1 <= i1

    def rows(start, d):
        return pl.ds(start, qb, stride=d) if d > 1 else pl.ds(pl.multiple_of(start, qb), qb)

    n_br = len(DILATIONS)
    for bi, d in enumerate(DILATIONS):
        n_blocks = seq // d // qb

        def sub_blocks(i, _, bi=bi, d=d, n_blocks=n_blocks):
            chains = [(h, i * DIL_UNROLL + u) for h in range(DIL_HEADS) for u in range(DIL_UNROLL)]
            cur, q, k, v, valid = [], [], [], [], []
            for h, g in chains:
                r = g // n_blocks
                n = g % n_blocks
                c0 = r + n * (qb * d)
                cur.append(c0)
                q.append(q_refs[h][rows(c0, d), :].astype(BF16))
                if n_blocks > 1:
                    p0 = r + jnp.maximum(n - 1, 0) * (qb * d)
                    k.append(jnp.concatenate([k_refs[h][rows(p0, d), :], k_refs[h][rows(c0, d), :]], axis=0))
                    v.append(jnp.concatenate([v_refs[h][rows(p0, d), :], v_refs[h][rows(c0, d), :]], axis=0))
                    valid.append(band & (j2 >= jnp.where(n > 0, 0, qb)))
                else:
                    k.append(k_refs[h][rows(c0, d), :])
                    v.append(v_refs[h][rows(c0, d), :])
                    valid.append(tril)
            idx = range(len(chains))
            s = [lax.dot_general(q[c], k[c].astype(BF16), nt, preferred_element_type=F32) for c in idx]
            s = [jnp.where(valid[c], s[c], -jnp.inf) for c in idx]
            m = [jnp.max(s[c], axis=1, keepdims=True) for c in idx]
            p = [jnp.exp(s[c] - m[c]) for c in idx]
            den = [jnp.sum(p[c], axis=1, keepdims=True) for c in idx]
            acc = [jnp.dot(p[c].astype(BF16), v[c].astype(BF16), preferred_element_type=F32) for c in idx]
            for c, (h, _) in enumerate(chains):
                acc_s[h * n_br + bi, rows(cur[c], d), :] = acc[c]
                m_s[h * n_br + bi, rows(cur[c], d), :] = jnp.broadcast_to(m[c], (qb, HEAD_DIM))
                den_s[h * n_br + bi, rows(cur[c], d), :] = jnp.broadcast_to(den[c], (qb, HEAD_DIM))
            return 0

        lax.fori_loop(0, seq // qb // DIL_UNROLL, sub_blocks, 0)

    def merge(t, _):
        rs = pl.ds(pl.multiple_of(t * qb, qb), qb)
        for h in range(DIL_HEADS):
            hs = slice(h * HEAD_DIM, (h + 1) * HEAD_DIM)
            ms = [m_s[h * n_br + bi, rs, :] for bi in range(n_br)]
            mx = functools.reduce(jnp.maximum, ms)
            w = [jnp.exp(m_b - mx) for m_b in ms]
            num = sum(w[bi] * acc_s[h * n_br + bi, rs, :] for bi in range(n_br))
            den = sum(w[bi] * den_s[h * n_br + bi, rs, :] for bi in range(n_br))
            o_ref[rs, hs] = _rms(num / den, g_ref[:, hs]).astype(o_ref.dtype)
        return 0

    lax.fori_loop(0, seq // qb, merge, 0)


def _dil_attention(proj, gain, batch, seq):
    width = DIL_HEADS * HEAD_DIM
    n_groups = D_DIL // width
    p3 = proj.reshape(batch, seq, proj.shape[-1])
    out = pl.pallas_call(
        _dil_kernel,
        grid=(batch, n_groups),
        in_specs=[
            pl.BlockSpec((None, seq, HEAD_DIM),
                         functools.partial(lambda b, g, *, off: (b, 0, off + g * DIL_HEADS),
                                           off=(3 * N_HEADS_SB + part * N_HEADS_DIL) + h))
            for part in range(3) for h in range(DIL_HEADS)
        ] + [pl.BlockSpec((1, width), lambda b, g: (0, g))],
        out_specs=pl.BlockSpec((None, seq, width), lambda b, g: (b, 0, g)),
        out_shape=jax.ShapeDtypeStruct((batch, seq, D_DIL), BF16),
        scratch_shapes=[pltpu.VMEM((3 * DIL_HEADS, seq, HEAD_DIM), F32)]
        + [pltpu.VMEM((DIL_HEADS * len(DILATIONS), seq, HEAD_DIM), F32)] * 3,
        compiler_params=pltpu.CompilerParams(
            dimension_semantics=("arbitrary", "arbitrary"), vmem_limit_bytes=VMEM_LIMIT),
        name="dil_attn",
    )(*([p3] * (3 * DIL_HEADS)), gain)
    return out.reshape(batch * seq, D_DIL)


def _out_kernel(sb_ref, dl_ref, w_ref, x_ref, gpost_ref, gpre_ref, x1_ref, h2_ref):
    mix = jnp.dot(sb_ref[...], w_ref[:D_SB, :], preferred_element_type=F32)
    mix += jnp.dot(dl_ref[...], w_ref[D_SB:, :], preferred_element_type=F32)
    x1 = x_ref[...] + _rms(mix, gpost_ref[...])
    x1_ref[...] = x1
    h2_ref[...] = _rms(x1, gpre_ref[...]).astype(BF16)


def _out_proj(mixed_sb, mixed_dl, w_out, x2, g_post, g_pre_ffn):
    T = x2.shape[0]
    tm = OUT_TM
    return pl.pallas_call(
        _out_kernel,
        grid=(T // tm,),
        in_specs=[
            pl.BlockSpec((tm, D_SB), lambda i: (i, 0)),
            pl.BlockSpec((tm, D_DIL), lambda i: (i, 0)),
            pl.BlockSpec((D_SB + D_DIL, D_MODEL), lambda i: (0, 0), pipeline_mode=pl.Buffered(1)),
            pl.BlockSpec((tm, D_MODEL), lambda i: (i, 0)),
            pl.BlockSpec((1, D_MODEL), lambda i: (0, 0)),
            pl.BlockSpec((1, D_MODEL), lambda i: (0, 0)),
        ],
        out_specs=[
            pl.BlockSpec((tm, D_MODEL), lambda i: (i, 0)),
            pl.BlockSpec((tm, D_MODEL), lambda i: (i, 0)),
        ],
        out_shape=[
            jax.ShapeDtypeStruct((T, D_MODEL), F32),
            jax.ShapeDtypeStruct((T, D_MODEL), BF16),
        ],
        compiler_params=pltpu.CompilerParams(
            dimension_semantics=("arbitrary",), vmem_limit_bytes=VMEM_LIMIT),
        name="out_proj",
    )(mixed_sb, mixed_dl, w_out, x2, g_post, g_pre_ffn)


def _ffn_kernel(h_ref, wg_ref, wv_ref, cwg_ref, cwv_ref, cbg_ref, cbv_ref, wdn_ref, x1_ref, g_ref,
                o_ref, tail_ref, *, tiles_per_seq):
    i = pl.program_id(0)
    f = pl.program_id(1)
    tm, halo = FFN_TM, FFN_HALO

    @pl.when((i == 0) & (f == 0))
    def _():
        tail_ref[...] = jnp.zeros_like(tail_ref)

    @pl.when(f == 0)
    def _():
        o_ref[...] = jnp.zeros_like(o_ref)

    keep = (i % tiles_per_seq) != 0

    def up_conv(part, w_ref, cw_ref, cb_ref):
        u = jnp.dot(h_ref[...], w_ref[...], preferred_element_type=F32)
        prev = jnp.where(keep, tail_ref[f, part], 0.0)
        tail_ref[f, part] = u[tm - halo:, :]
        u = jnp.concatenate([prev, u], axis=0)
        cw = cw_ref[...]
        conv = cb_ref[...] + cw[CONV_WIDTH - 1:CONV_WIDTH, :] * u[halo:, :]
        for tap in range(CONV_WIDTH - 1):
            back = CONV_WIDTH - 1 - tap
            conv += cw[tap:tap + 1, :] * u[halo - back:halo - back + tm, :]
        return conv

    gate = up_conv(0, wg_ref, cwg_ref, cbg_ref)
    value = up_conv(1, wv_ref, cwv_ref, cbv_ref)
    y = jax.nn.gelu(gate, approximate=True) * value
    o_ref[...] += jnp.dot(y.astype(BF16), wdn_ref[...], preferred_element_type=F32)

    @pl.when(f == pl.num_programs(1) - 1)
    def _():
        o_ref[...] = x1_ref[...] + _rms(o_ref[...], g_ref[...])


def _conv_ffn(h2, w_up_p, conv_w_p, conv_b_p, w_down_p, x1, g_post, seq):
    T = h2.shape[0]
    tm, fc, halo = FFN_TM, FFN_FC, FFN_HALO
    n_chunks = D_FF_PAD // fc
    gate_blk = lambda i, f: (0, f)
    value_blk = lambda i, f: (0, n_chunks + f)
    return pl.pallas_call(
        functools.partial(_ffn_kernel, tiles_per_seq=seq // tm),
        grid=(T // tm, n_chunks),
        in_specs=[
            pl.BlockSpec((tm, D_MODEL), lambda i, f: (i, 0), pipeline_mode=pl.Buffered(1)),
            pl.BlockSpec((D_MODEL, fc), gate_blk),
            pl.BlockSpec((D_MODEL, fc), value_blk),
            pl.BlockSpec((CONV_WIDTH, fc), gate_blk),
            pl.BlockSpec((CONV_WIDTH, fc), value_blk),
            pl.BlockSpec((1, fc), gate_blk),
            pl.BlockSpec((1, fc), value_blk),
            pl.BlockSpec((fc, D_MODEL), lambda i, f: (f, 0)),
            pl.BlockSpec((tm, D_MODEL), lambda i, f: (i, 0), pipeline_mode=pl.Buffered(1)),
            pl.BlockSpec((1, D_MODEL), lambda i, f: (0, 0)),
        ],
        out_specs=pl.BlockSpec((tm, D_MODEL), lambda i, f: (i, 0)),
        out_shape=jax.ShapeDtypeStruct((T, D_MODEL), F32),
        scratch_shapes=[pltpu.VMEM((n_chunks, 2, halo, fc), F32)],
        compiler_params=pltpu.CompilerParams(
            dimension_semantics=("arbitrary", "arbitrary"), vmem_limit_bytes=VMEM_LIMIT),
        name="conv_ffn",
    )(h2, w_up_p, w_up_p, conv_w_p, conv_w_p, conv_b_p, conv_b_p, w_down_p, x1, g_post)


def _pad_gate_value(a, dtype):
    pad = ((0, 0), (0, D_FF_PAD - D_FF))
    return jnp.concatenate([jnp.pad(a[:, :D_FF].astype(dtype), pad), jnp.pad(a[:, D_FF:].astype(dtype), pad)], axis=1)


def kernel(x, pre_mix_gain, post_mix_gain, pre_ffn_gain, post_ffn_gain, w_in, sb_out_gain, dil_out_gain,
           w_out, w_up, conv_w, conv_b, w_down):
    batch, seq, _ = x.shape
    depth = w_in.shape[0]
    ta, tb = _section_tables(seq)
    x2 = x.reshape(batch * seq, D_MODEL)
    for layer in range(depth):
        row = lambda g: g[layer][None, :]
        proj = _qkv_proj(x2, row(pre_mix_gain), w_in[layer].astype(BF16), ta, tb, seq)
        mixed_sb = _sb_attention(proj, row(sb_out_gain), batch, seq)
        mixed_dl = _dil_attention(proj, row(dil_out_gain), batch, seq)
        x1, h2 = _out_proj(mixed_sb, mixed_dl, w_out[layer].astype(BF16), x2,
                           row(post_mix_gain), row(pre_ffn_gain))
        w_down_p = jnp.pad(w_down[layer].astype(BF16), ((0, D_FF_PAD - D_FF), (0, 0)))
        x2 = _conv_ffn(h2, _pad_gate_value(w_up[layer], BF16), _pad_gate_value(conv_w[layer], F32),
                       _pad_gate_value(row(conv_b), F32), w_down_p, x1, row(post_ffn_gain), seq)
    return x2.reshape(batch, seq, D_MODEL)
```

```python
import functools
import math

import jax
import jax.numpy as jnp
from jax import lax
from jax.experimental import pallas as pl
from jax.experimental.pallas import tpu as pltpu

F32 = jnp.float32
BF16 = jnp.bfloat16

D_MODEL = 2048
HEAD_DIM = 128
N_HEADS_SB = 8
N_HEADS_DIL = 8
D_SB = N_HEADS_SB * HEAD_DIM
D_DIL = N_HEADS_DIL * HEAD_DIM
DILATIONS = (1, 4, 16)
N_BACK = 128
D_FF = 5504
CONV_WIDTH = 3
ROPE_THETA = 10000.0
RMS_EPS = 1e-6
SCALE = HEAD_DIM ** -0.5
LOG2_E = math.log2(math.e)

LANES = 128
VMEM_LIMIT = 56 * 1024 * 1024

PROJ_TM = 1024
OUT_TM = 512
SECTION = 1024
SB_HEADS = 4
SB_TQ = 256
SB_TK = 128
SB_DEAD_LOG2 = -152.0
DIL_HEADS = 2
DIL_QB = 128
DIL_UNROLL = 4
FFN_TM = 1024
FFN_FC = 512
FFN_HALO = 8
D_FF_PAD = -(-D_FF // FFN_FC) * FFN_FC


def _rms(x, gain):
    return x * lax.rsqrt(jnp.mean(x * x, axis=-1, keepdims=True) + RMS_EPS) * gain


def _qkv_kernel(x_ref, g_ref, w_ref, ta_ref, tb_ref, o_ref, h_ref):
    j = pl.program_id(1)

    @pl.when(j == 0)
    def _():
        h_ref[...] = _rms(x_ref[...], g_ref[...]).astype(BF16)

    acc = jnp.dot(h_ref[...], w_ref[...], preferred_element_type=F32)
    ta, tb = ta_ref[...], tb_ref[...]
    for h in range(SECTION // HEAD_DIM):
        hs = slice(h * HEAD_DIM, (h + 1) * HEAD_DIM)
        a_h = acc[:, hs]
        o_ref[:, hs] = (a_h * ta + pltpu.roll(a_h, HEAD_DIM // 2, axis=1) * tb).astype(BF16)


def _section_tables(seq):
    inv_freq = ROPE_THETA ** (-jnp.arange(0, HEAD_DIM, 2, dtype=F32) / HEAD_DIM)
    ang = jnp.arange(seq, dtype=F32)[:, None] * inv_freq[None, :]
    cos, sin = jnp.cos(ang), jnp.sin(ang)
    cos_t = jnp.concatenate([cos, cos], axis=-1)
    sin_t = jnp.concatenate([-sin, sin], axis=-1)
    one, zero = jnp.ones_like(cos_t), jnp.zeros_like(cos_t)
    ta = jnp.stack([cos_t * SCALE, cos_t, one * (SCALE * LOG2_E), one])
    tb = jnp.stack([sin_t * SCALE, sin_t, zero, zero])
    return ta, tb


def _qkv_proj(x2, gain, w_in, ta, tb, seq):
    T = x2.shape[0]
    tm = PROJ_TM
    n_pos_blocks = seq // tm

    def table_blk(i, j):
        kind = jnp.where(j == 3, 0, jnp.where(j == 4, 1, jnp.where(j == 0, 2, 3)))
        return (kind, i % n_pos_blocks, 0)

    return pl.pallas_call(
        _qkv_kernel,
        grid=(T // tm, 6),
        in_specs=[
            pl.BlockSpec((tm, D_MODEL), lambda i, j: (i, 0)),
            pl.BlockSpec((1, D_MODEL), lambda i, j: (0, 0)),
            pl.BlockSpec((D_MODEL, SECTION), lambda i, j: (0, j)),
            pl.BlockSpec((None, tm, HEAD_DIM), table_blk),
            pl.BlockSpec((None, tm, HEAD_DIM), table_blk),
        ],
        out_specs=pl.BlockSpec((tm, SECTION), lambda i, j: (i, j)),
        out_shape=jax.ShapeDtypeStruct((T, 3 * (D_SB + D_DIL)), BF16),
        scratch_shapes=[pltpu.VMEM((tm, D_MODEL), BF16)],
        compiler_params=pltpu.CompilerParams(
            dimension_semantics=("arbitrary", "arbitrary"), vmem_limit_bytes=VMEM_LIMIT),
        name="qkv_proj",
    )(x2, gain, w_in, ta, tb)


def _sb_kernel(q_ref, k_ref, v_ref, g_ref, uu_ref, o_ref, c_ref, acc_ref):
    seq = q_ref.shape[0]
    tq, tk = SB_TQ, SB_TK
    nt = (((1,), (1,)), ((), ()))
    heads = range(SB_HEADS)
    hs = [slice(h * HEAD_DIM, (h + 1) * HEAD_DIM) for h in heads]

    def aligned(row):
        return row if isinstance(row, int) else pl.multiple_of(row, tq)

    def key_tile(qb, k_start, width, ends_on_diagonal):
        n_sub = width // tk
        q0 = aligned(qb * tq)
        k0 = aligned(k_start)
        z = [lax.dot_general(q_ref[pl.ds(q0, tq), hs[h]], k_ref[pl.ds(k0, width), hs[h]], nt,
                             preferred_element_type=F32) for h in heads]
        log_beta = [jnp.minimum(z[h], 0.0) - jnp.log2(1.0 + jnp.exp2(-jnp.abs(z[h]))) for h in heads]
        log_keep = [log_beta[h] - z[h] for h in heads]
        masked = ends_on_diagonal
        if masked:
            causal = (lax.broadcasted_iota(jnp.int32, (tq, width), 0) + (width - tq)
                      > lax.broadcasted_iota(jnp.int32, (tq, width), 1))
            log_keep = [jnp.where(causal, lk, 0.0) for lk in log_keep]
        hi = [lk.astype(BF16) for lk in log_keep]
        lo = [(log_keep[h] - hi[h].astype(F32)).astype(BF16) for h in heads]
        r = [[jnp.dot(jnp.concatenate([hi[h][:, j * tk:(j + 1) * tk], lo[h][:, j * tk:(j + 1) * tk]], axis=1),
                      uu_ref[...], preferred_element_type=F32) for j in range(n_sub)] for h in heads]
        pv = []
        for h in heads:
            c = c_ref[h]
            a = [None] * n_sub
            for j in range(n_sub - 1, -1, -1):
                a[j] = jnp.exp2(log_beta[h][:, j * tk:(j + 1) * tk] + r[h][j][:, :tk] + c)
                c = c + r[h][j][:, tk:]
            c_ref[h] = c
            a = jnp.concatenate(a, axis=1)
            if masked:
                a = jnp.where(causal, a, 0.0)
            pv.append(jnp.dot(a.astype(BF16), v_ref[pl.ds(k0, width), hs[h]], preferred_element_type=F32))
        for h in heads:
            acc_ref[h] += pv[h]

    def c_max():
        return jnp.max(functools.reduce(jnp.maximum, [c_ref[h] for h in heads]))

    def finish(qb):
        q0 = aligned(qb * tq)
        for h in heads:
            o_ref[pl.ds(q0, tq), hs[h]] = _rms(acc_ref[h], g_ref[:, hs[h]]).astype(o_ref.dtype)

    def q_tile(qb, _):
        c_ref[...] = jnp.zeros_like(c_ref)
        acc_ref[...] = jnp.zeros_like(acc_ref)
        key_tile(qb, (qb - 1) * tq, 2 * tq, True)

        def live(carry):
            kt, c_hi = carry
            return (kt >= 0) & (c_hi > SB_DEAD_LOG2)

        def earlier(carry):
            kt, _ = carry
            key_tile(qb, kt * tq, tq, False)
            return kt - 1, c_max()

        lax.while_loop(live, earlier, (qb - 2, c_max()))
        finish(qb)
        return 0

    c_ref[...] = jnp.zeros_like(c_ref)
    acc_ref[...] = jnp.zeros_like(acc_ref)
    key_tile(0, 0, tq, True)
    finish(0)
    lax.fori_loop(1, seq // tq, q_tile, 0)


def _sb_attention(proj, gain, batch, seq):
    width = SB_HEADS * HEAD_DIM
    n_groups = D_SB // width
    p3 = proj.reshape(batch, seq, proj.shape[-1])
    tri = (jnp.arange(SB_TK)[:, None] > jnp.arange(SB_TK)[None, :])
    u = jnp.concatenate([tri.astype(BF16), jnp.ones((SB_TK, SB_TK), BF16)], axis=1)
    uu = jnp.concatenate([u, u], axis=0)
    out = pl.pallas_call(
        _sb_kernel,
        grid=(batch, n_groups),
        in_specs=[
            pl.BlockSpec((None, seq, width), lambda b, g: (b, 0, g)),
            pl.BlockSpec((None, seq, width), lambda b, g: (b, 0, n_groups + g)),
            pl.BlockSpec((None, seq, width), lambda b, g: (b, 0, 2 * n_groups + g)),
            pl.BlockSpec((1, width), lambda b, g: (0, g)),
            pl.BlockSpec((2 * SB_TK, 2 * SB_TK), lambda b, g: (0, 0)),
        ],
        out_specs=pl.BlockSpec((None, seq, width), lambda b, g: (b, 0, g)),
        out_shape=jax.ShapeDtypeStruct((batch, seq, D_SB), BF16),
        scratch_shapes=[pltpu.VMEM((SB_HEADS, SB_TQ, SB_TK), F32),
                        pltpu.VMEM((SB_HEADS, SB_TQ, HEAD_DIM), F32)],
        compiler_params=pltpu.CompilerParams(
            dimension_semantics=("arbitrary", "arbitrary"), vmem_limit_bytes=VMEM_LIMIT),
        name="sb_attn",
    )(p3, p3, p3, gain, uu)
    return out.reshape(batch * seq, D_SB)


def _dil_kernel(*refs):
    g_ref, o_ref, qkv_s, out_s, lse_s = refs[3 * DIL_HEADS:]
    seq = o_ref.shape[0]
    for i in range(3 * DIL_HEADS):
        qkv_s[i] = refs[i][...].astype(F32)
    q_refs, k_refs, v_refs = ([qkv_s.at[p * DIL_HEADS + h] for h in range(DIL_HEADS)] for p in range(3))
    qb = DIL_QB
    nt = (((1,), (1,)), ((), ()))
    i2 = lax.broadcasted_iota(jnp.int32, (qb, 2 * qb), 0)
    j2 = lax.broadcasted_iota(jnp.int32, (qb, 2 * qb), 1)
    band = (j2 >= i2) & (j2 <= i2 + N_BACK)
    i1 = lax.broadcasted_iota(jnp.int32, (qb, qb), 0)
    j1 = lax.broadcasted_iota(jnp.int32, (qb, qb), 1)
    tril = j1 <= i1

    def rows(start, d):
        return pl.ds(start, qb, stride=d) if d > 1 else pl.ds(pl.multiple_of(start, qb), qb)

    n_br = len(DILATIONS)
    for bi, d in enumerate(DILATIONS):
        n_blocks = seq // d // qb

        def sub_blocks(i, _, bi=bi, d=d, n_blocks=n_blocks):
            chains = [(h, i * DIL_UNROLL + u) for h in range(DIL_HEADS) for u in range(DIL_UNROLL)]
            cur, q, k, v, valid = [], [], [], [], []
            for h, g in chains:
                r = g // n_blocks
                n = g % n_blocks
                c0 = r + n * (qb * d)
                cur.append(c0)
                q.append(q_refs[h][rows(c0, d), :].astype(BF16))
                if n_blocks > 1:
                    p0 = r + jnp.maximum(n - 1, 0) * (qb * d)
                    k.append(jnp.concatenate([k_refs[h][rows(p0, d), :], k_refs[h][rows(c0, d), :]], axis=0))
                    v.append(jnp.concatenate([v_refs[h][rows(p0, d), :], v_refs[h][rows(c0, d), :]], axis=0))
                    valid.append(band & (j2 >= jnp.where(n > 0, 0, qb)))
                else:
                    k.append(k_refs[h][rows(c0, d), :])
                    v.append(v_refs[h][rows(c0, d), :])
                    valid.append(tril)
            idx = range(len(chains))
            s = [lax.dot_general(q[c], k[c].astype(BF16), nt, preferred_element_type=F32) for c in idx]
            s = [jnp.where(valid[c], s[c], -jnp.inf) for c in idx]
            m = [jnp.max(s[c], axis=1, keepdims=True) for c in idx]
            p = [jnp.exp(s[c] - m[c]) for c in idx]
            den = [jnp.sum(p[c], axis=1, keepdims=True) for c in idx]
            acc = [jnp.dot(p[c].astype(BF16), v[c].astype(BF16), preferred_element_type=F32) for c in idx]
            for c, (h, _) in enumerate(chains):
                out_s[h * n_br + bi, rows(cur[c], d), :] = acc[c] * (1.0 / den[c])
                lse_s[h * n_br + bi, rows(cur[c], d), :] = jnp.broadcast_to(m[c] + jnp.log(den[c]), (qb, HEAD_DIM))
            return 0

        lax.fori_loop(0, seq // qb // DIL_UNROLL, sub_blocks, 0)

    def merge(t, _):
        rs = pl.ds(pl.multiple_of(t * qb, qb), qb)
        for h in range(DIL_HEADS):
            hs = slice(h * HEAD_DIM, (h + 1) * HEAD_DIM)
            lse = [lse_s[h * n_br + bi, rs, :] for bi in range(n_br)]
            mx = functools.reduce(jnp.maximum, lse)
            w = [jnp.exp(l - mx) for l in lse]
            num = sum(w[bi] * out_s[h * n_br + bi, rs, :] for bi in range(n_br))
            o_ref[rs, hs] = _rms(num / sum(w), g_ref[:, hs]).astype(o_ref.dtype)
        return 0

    lax.fori_loop(0, seq // qb, merge, 0)


def _dil_attention(proj, gain, batch, seq):
    width = DIL_HEADS * HEAD_DIM
    n_groups = D_DIL // width
    p3 = proj.reshape(batch, seq, proj.shape[-1])
    out = pl.pallas_call(
        _dil_kernel,
        grid=(batch, n_groups),
        in_specs=[
            pl.BlockSpec((None, seq, HEAD_DIM),
                         functools.partial(lambda b, g, *, off: (b, 0, off + g * DIL_HEADS),
                                           off=(3 * N_HEADS_SB + part * N_HEADS_DIL) + h))
            for part in range(3) for h in range(DIL_HEADS)
        ] + [pl.BlockSpec((1, width), lambda b, g: (0, g))],
        out_specs=pl.BlockSpec((None, seq, width), lambda b, g: (b, 0, g)),
        out_shape=jax.ShapeDtypeStruct((batch, seq, D_DIL), BF16),
        scratch_shapes=[pltpu.VMEM((3 * DIL_HEADS, seq, HEAD_DIM), F32)]
        + [pltpu.VMEM((DIL_HEADS * len(DILATIONS), seq, HEAD_DIM), F32)] * 2,
        compiler_params=pltpu.CompilerParams(
            dimension_semantics=("arbitrary", "arbitrary"), vmem_limit_bytes=VMEM_LIMIT),
        name="dil_attn",
    )(*([p3] * (3 * DIL_HEADS)), gain)
    return out.reshape(batch * seq, D_DIL)


def _out_kernel(sb_ref, dl_ref, w_ref, x_ref, gpost_ref, gpre_ref, x1_ref, h2_ref):
    mix = jnp.dot(sb_ref[...], w_ref[:D_SB, :], preferred_element_type=F32)
    mix += jnp.dot(dl_ref[...], w_ref[D_SB:, :], preferred_element_type=F32)
    x1 = x_ref[...] + _rms(mix, gpost_ref[...])
    x1_ref[...] = x1
    h2_ref[...] = _rms(x1, gpre_ref[...]).astype(BF16)


def _out_proj(mixed_sb, mixed_dl, w_out, x2, g_post, g_pre_ffn):
    T = x2.shape[0]
    tm = OUT_TM
    return pl.pallas_call(
        _out_kernel,
        grid=(T // tm,),
        in_specs=[
            pl.BlockSpec((tm, D_SB), lambda i: (i, 0)),
            pl.BlockSpec((tm, D_DIL), lambda i: (i, 0)),
            pl.BlockSpec((D_SB + D_DIL, D_MODEL), lambda i: (0, 0), pipeline_mode=pl.Buffered(1)),
            pl.BlockSpec((tm, D_MODEL), lambda i: (i, 0)),
            pl.BlockSpec((1, D_MODEL), lambda i: (0, 0)),
            pl.BlockSpec((1, D_MODEL), lambda i: (0, 0)),
        ],
        out_specs=[
            pl.BlockSpec((tm, D_MODEL), lambda i: (i, 0)),
            pl.BlockSpec((tm, D_MODEL), lambda i: (i, 0)),
        ],
        out_shape=[
            jax.ShapeDtypeStruct((T, D_MODEL), F32),
            jax.ShapeDtypeStruct((T, D_MODEL), BF16),
        ],
        compiler_params=pltpu.CompilerParams(
            dimension_semantics=("arbitrary",), vmem_limit_bytes=VMEM_LIMIT),
        name="out_proj",
    )(mixed_sb, mixed_dl, w_out, x2, g_post, g_pre_ffn)


def _ffn_kernel(h_ref, wg_ref, wv_ref, cwg_ref, cwv_ref, cbg_ref, cbv_ref, wdn_ref, x1_ref, g_ref,
                o_ref, tail_ref, *, tiles_per_seq):
    i = pl.program_id(0)
    f = pl.program_id(1)
    tm, halo = FFN_TM, FFN_HALO

    @pl.when((i == 0) & (f == 0))
    def _():
        tail_ref[...] = jnp.zeros_like(tail_ref)

    @pl.when(f == 0)
    def _():
        o_ref[...] = jnp.zeros_like(o_ref)

    keep = (i % tiles_per_seq) != 0

    def up_conv(part, w_ref, cw_ref, cb_ref):
        u = jnp.dot(h_ref[...], w_ref[...], preferred_element_type=F32)
        prev = jnp.where(keep, tail_ref[f, part], 0.0)
        tail_ref[f, part] = u[tm - halo:, :]
        u = jnp.concatenate([prev, u], axis=0)
        cw = cw_ref[...]
        conv = cb_ref[...] + cw[CONV_WIDTH - 1:CONV_WIDTH, :] * u[halo:, :]
        for tap in range(CONV_WIDTH - 1):
            back = CONV_WIDTH - 1 - tap
            conv += cw[tap:tap + 1, :] * u[halo - back:halo - back + tm, :]
        return conv

    gate = up_conv(0, wg_ref, cwg_ref, cbg_ref)
    value = up_conv(1, wv_ref, cwv_ref, cbv_ref)
    y = jax.nn.gelu(gate, approximate=True) * value
    o_ref[...] += jnp.dot(y.astype(BF16), wdn_ref[...], preferred_element_type=F32)

    @pl.when(f == pl.num_programs(1) - 1)
    def _():
        o_ref[...] = x1_ref[...] + _rms(o_ref[...], g_ref[...])


def _conv_ffn(h2, w_up_p, conv_w_p, conv_b_p, w_down_p, x1, g_post, seq):
    T = h2.shape[0]
    tm, fc, halo = FFN_TM, FFN_FC, FFN_HALO
    n_chunks = D_FF_PAD // fc
    gate_blk = lambda i, f: (0, f)
    value_blk = lambda i, f: (0, n_chunks + f)
    return pl.pallas_call(
        functools.partial(_ffn_kernel, tiles_per_seq=seq // tm),
        grid=(T // tm, n_chunks),
        in_specs=[
            pl.BlockSpec((tm, D_MODEL), lambda i, f: (i, 0), pipeline_mode=pl.Buffered(1)),
            pl.BlockSpec((D_MODEL, fc), gate_blk),
            pl.BlockSpec((D_MODEL, fc), value_blk),
            pl.BlockSpec((CONV_WIDTH, fc), gate_blk),
            pl.BlockSpec((CONV_WIDTH, fc), value_blk),
            pl.BlockSpec((1, fc), gate_blk),
            pl.BlockSpec((1, fc), value_blk),
            pl.BlockSpec((fc, D_MODEL), lambda i, f: (f, 0)),
            pl.BlockSpec((tm, D_MODEL), lambda i, f: (i, 0), pipeline_mode=pl.Buffered(1)),
            pl.BlockSpec((1, D_MODEL), lambda i, f: (0, 0)),
        ],
        out_specs=pl.BlockSpec((tm, D_MODEL), lambda i, f: (i, 0)),
        out_shape=jax.ShapeDtypeStruct((T, D_MODEL), F32),
        scratch_shapes=[pltpu.VMEM((n_chunks, 2, halo, fc), F32)],
        compiler_params=pltpu.CompilerParams(
            dimension_semantics=("arbitrary", "arbitrary"), vmem_limit_bytes=VMEM_LIMIT),
        name="conv_ffn",
    )(h2, w_up_p, w_up_p, conv_w_p, conv_w_p, conv_b_p, conv_b_p, w_down_p, x1, g_post)


def _pad_gate_value(a, dtype):
    zeros = jnp.zeros((a.shape[0], D_FF_PAD - D_FF), dtype)
    return jnp.concatenate([a[:, :D_FF].astype(dtype), zeros, a[:, D_FF:].astype(dtype), zeros], axis=1)


def kernel(x, pre_mix_gain, post_mix_gain, pre_ffn_gain, post_ffn_gain, w_in, sb_out_gain, dil_out_gain,
           w_out, w_up, conv_w, conv_b, w_down):
    batch, seq, _ = x.shape
    depth = w_in.shape[0]
    ta, tb = _section_tables(seq)
    x2 = x.reshape(batch * seq, D_MODEL)
    for layer in range(depth):
        row = lambda g: g[layer][None, :]
        proj = _qkv_proj(x2, row(pre_mix_gain), w_in[layer].astype(BF16), ta, tb, seq)
        mixed_sb = _sb_attention(proj, row(sb_out_gain), batch, seq)
        mixed_dl = _dil_attention(proj, row(dil_out_gain), batch, seq)
        x1, h2 = _out_proj(mixed_sb, mixed_dl, w_out[layer].astype(BF16), x2,
                           row(post_mix_gain), row(pre_ffn_gain))
        w_down_p = jnp.concatenate([w_down[layer].astype(BF16), jnp.zeros((D_FF_PAD - D_FF, D_MODEL), BF16)], axis=0)
        x2 = _conv_ffn(h2, _pad_gate_value(w_up[layer], BF16), _pad_gate_value(conv_w[layer], F32),
                       _pad_gate_value(row(conv_b), F32), w_down_p, x1, row(post_ffn_gain), seq)
    return x2.reshape(batch, seq, D_MODEL)
```

```python
import functools
import math

import jax
import jax.numpy as jnp
from jax import lax
from jax.experimental import pallas as pl
from jax.experimental.pallas import tpu as pltpu

F32 = jnp.float32
BF16 = jnp.bfloat16

D_MODEL = 2048
HEAD_DIM = 128
N_HEADS_SB = 8
N_HEADS_DIL = 8
D_SB = N_HEADS_SB * HEAD_DIM
D_DIL = N_HEADS_DIL * HEAD_DIM
DILATIONS = (1, 4, 16)
N_BACK = 128
D_FF = 5504
CONV_WIDTH = 3
ROPE_THETA = 10000.0
RMS_EPS = 1e-6
SCALE = HEAD_DIM ** -0.5
LOG2_E = math.log2(math.e)

LANES = 128
VMEM_LIMIT = 56 * 1024 * 1024
FFN_VMEM_LIMIT = 60 * 1024 * 1024

PROJ_TM = 1024
OUT_TM = 512
SECTION = 1024
SB_HEADS = 4
SB_TQ = 256
SB_TK = 128
SB_DEAD_LOG2 = -152.0
DIL_HEADS = 2
DIL_QB = 128
DIL_UNROLL = 4
FFN_TM = 1024
FFN_FC = 512
FFN_HALO = 8
D_FF_PAD = -(-D_FF // FFN_FC) * FFN_FC
STAGE_UP_ROWS = 128
STAGE_DOWN_COLS = 256


def _rms(x, gain):
    return x * lax.rsqrt(jnp.mean(x * x, axis=-1, keepdims=True) + RMS_EPS) * gain


def _qkv_kernel(x_ref, g_ref, w_ref, ta_ref, tb_ref, o_ref, h_ref):
    j = pl.program_id(1)

    @pl.when(j == 0)
    def _():
        h_ref[...] = _rms(x_ref[...], g_ref[...]).astype(BF16)

    acc = jnp.dot(h_ref[...], w_ref[...], preferred_element_type=F32)
    ta, tb = ta_ref[...], tb_ref[...]
    for h in range(SECTION // HEAD_DIM):
        hs = slice(h * HEAD_DIM, (h + 1) * HEAD_DIM)
        a_h = acc[:, hs]
        o_ref[:, hs] = (a_h * ta + pltpu.roll(a_h, HEAD_DIM // 2, axis=1) * tb).astype(BF16)


def _section_tables(seq):
    inv_freq = ROPE_THETA ** (-jnp.arange(0, HEAD_DIM, 2, dtype=F32) / HEAD_DIM)
    ang = jnp.arange(seq, dtype=F32)[:, None] * inv_freq[None, :]
    cos, sin = jnp.cos(ang), jnp.sin(ang)
    cos_t = jnp.concatenate([cos, cos], axis=-1)
    sin_t = jnp.concatenate([-sin, sin], axis=-1)
    one, zero = jnp.ones_like(cos_t), jnp.zeros_like(cos_t)
    ta = jnp.stack([cos_t * SCALE, cos_t, one * (SCALE * LOG2_E), one])
    tb = jnp.stack([sin_t * SCALE, sin_t, zero, zero])
    return ta, tb


def _qkv_proj(x2, gain, w_in, ta, tb, seq):
    T = x2.shape[0]
    tm = PROJ_TM
    n_pos_blocks = seq // tm

    def table_blk(i, j):
        kind = jnp.where(j == 3, 0, jnp.where(j == 4, 1, jnp.where(j == 0, 2, 3)))
        return (kind, i % n_pos_blocks, 0)

    return pl.pallas_call(
        _qkv_kernel,
        grid=(T // tm, 6),
        in_specs=[
            pl.BlockSpec((tm, D_MODEL), lambda i, j: (i, 0)),
            pl.BlockSpec((1, D_MODEL), lambda i, j: (0, 0)),
            pl.BlockSpec((D_MODEL, SECTION), lambda i, j: (0, j)),
            pl.BlockSpec((None, tm, HEAD_DIM), table_blk),
            pl.BlockSpec((None, tm, HEAD_DIM), table_blk),
        ],
        out_specs=pl.BlockSpec((tm, SECTION), lambda i, j: (i, j)),
        out_shape=jax.ShapeDtypeStruct((T, 3 * (D_SB + D_DIL)), BF16),
        scratch_shapes=[pltpu.VMEM((tm, D_MODEL), BF16)],
        compiler_params=pltpu.CompilerParams(
            dimension_semantics=("arbitrary", "arbitrary"), vmem_limit_bytes=VMEM_LIMIT),
        name="qkv_proj",
    )(x2, gain, w_in, ta, tb)


def _sb_kernel(q_ref, k_ref, v_ref, g_ref, uu_ref, o_ref, c_ref, acc_ref):
    seq = q_ref.shape[0]
    tq, tk = SB_TQ, SB_TK
    nt = (((1,), (1,)), ((), ()))
    heads = range(SB_HEADS)
    hs = [slice(h * HEAD_DIM, (h + 1) * HEAD_DIM) for h in heads]

    def aligned(row):
        return row if isinstance(row, int) else pl.multiple_of(row, tq)

    def key_tile(qb, k_start, width, ends_on_diagonal):
        n_sub = width // tk
        q0 = aligned(qb * tq)
        k0 = aligned(k_start)
        z = [lax.dot_general(q_ref[pl.ds(q0, tq), hs[h]], k_ref[pl.ds(k0, width), hs[h]], nt,
                             preferred_element_type=F32) for h in heads]
        log_beta = [jnp.minimum(z[h], 0.0) - jnp.log2(1.0 + jnp.exp2(-jnp.abs(z[h]))) for h in heads]
        log_keep = [log_beta[h] - z[h] for h in heads]
        masked = ends_on_diagonal
        if masked:
            causal = (lax.broadcasted_iota(jnp.int32, (tq, width), 0) + (width - tq)
                      > lax.broadcasted_iota(jnp.int32, (tq, width), 1))
            log_keep = [jnp.where(causal, lk, 0.0) for lk in log_keep]
        hi = [lk.astype(BF16) for lk in log_keep]
        lo = [(log_keep[h] - hi[h].astype(F32)).astype(BF16) for h in heads]
        r = [[jnp.dot(jnp.concatenate([hi[h][:, j * tk:(j + 1) * tk], lo[h][:, j * tk:(j + 1) * tk]], axis=1),
                      uu_ref[...], preferred_element_type=F32) for j in range(n_sub)] for h in heads]
        pv = []
        for h in heads:
            c = c_ref[h]
            a = [None] * n_sub
            for j in range(n_sub - 1, -1, -1):
                a[j] = jnp.exp2(log_beta[h][:, j * tk:(j + 1) * tk] + r[h][j][:, :tk] + c)
                c = c + r[h][j][:, tk:]
            c_ref[h] = c
            a = jnp.concatenate(a, axis=1)
            if masked:
                a = jnp.where(causal, a, 0.0)
            pv.append(jnp.dot(a.astype(BF16), v_ref[pl.ds(k0, width), hs[h]], preferred_element_type=F32))
        for h in heads:
            acc_ref[h] += pv[h]

    def c_max():
        return jnp.max(functools.reduce(jnp.maximum, [c_ref[h] for h in heads]))

    def finish(qb):
        q0 = aligned(qb * tq)
        for h in heads:
            o_ref[pl.ds(q0, tq), hs[h]] = _rms(acc_ref[h], g_ref[:, hs[h]]).astype(o_ref.dtype)

    def q_tile(qb, _):
        c_ref[...] = jnp.zeros_like(c_ref)
        acc_ref[...] = jnp.zeros_like(acc_ref)
        key_tile(qb, (qb - 1) * tq, 2 * tq, True)

        def live(carry):
            kt, c_hi = carry
            return (kt >= 0) & (c_hi > SB_DEAD_LOG2)

        def earlier(carry):
            kt, _ = carry
            key_tile(qb, kt * tq, tq, False)
            return kt - 1, c_max()

        lax.while_loop(live, earlier, (qb - 2, c_max()))
        finish(qb)
        return 0

    c_ref[...] = jnp.zeros_like(c_ref)
    acc_ref[...] = jnp.zeros_like(acc_ref)
    key_tile(0, 0, tq, True)
    finish(0)
    lax.fori_loop(1, seq // tq, q_tile, 0)


def _sb_attention(proj, gain, batch, seq):
    width = SB_HEADS * HEAD_DIM
    n_groups = D_SB // width
    p3 = proj.reshape(batch, seq, proj.shape[-1])
    tri = (jnp.arange(SB_TK)[:, None] > jnp.arange(SB_TK)[None, :])
    u = jnp.concatenate([tri.astype(BF16), jnp.ones((SB_TK, SB_TK), BF16)], axis=1)
    uu = jnp.concatenate([u, u], axis=0)
    out = pl.pallas_call(
        _sb_kernel,
        grid=(batch, n_groups),
        in_specs=[
            pl.BlockSpec((None, seq, width), lambda b, g: (b, 0, g)),
            pl.BlockSpec((None, seq, width), lambda b, g: (b, 0, n_groups + g)),
            pl.BlockSpec((None, seq, width), lambda b, g: (b, 0, 2 * n_groups + g)),
            pl.BlockSpec((1, width), lambda b, g: (0, g)),
            pl.BlockSpec((2 * SB_TK, 2 * SB_TK), lambda b, g: (0, 0)),
        ],
        out_specs=pl.BlockSpec((None, seq, width), lambda b, g: (b, 0, g)),
        out_shape=jax.ShapeDtypeStruct((batch, seq, D_SB), BF16),
        scratch_shapes=[pltpu.VMEM((SB_HEADS, SB_TQ, SB_TK), F32),
                        pltpu.VMEM((SB_HEADS, SB_TQ, HEAD_DIM), F32)],
        compiler_params=pltpu.CompilerParams(
            dimension_semantics=("arbitrary", "arbitrary"), vmem_limit_bytes=VMEM_LIMIT),
        name="sb_attn",
    )(p3, p3, p3, gain, uu)
    return out.reshape(batch * seq, D_SB)


def _dil_kernel(*refs):
    g_ref, o_ref, qkv_s, out_s, lse_s = refs[3 * DIL_HEADS:]
    seq = o_ref.shape[0]
    for i in range(3 * DIL_HEADS):
        qkv_s[i] = refs[i][...].astype(F32)
    q_refs, k_refs, v_refs = ([qkv_s.at[p * DIL_HEADS + h] for h in range(DIL_HEADS)] for p in range(3))
    qb = DIL_QB
    nt = (((1,), (1,)), ((), ()))
    i2 = lax.broadcasted_iota(jnp.int32, (qb, 2 * qb), 0)
    j2 = lax.broadcasted_iota(jnp.int32, (qb, 2 * qb), 1)
    band = (j2 >= i2) & (j2 <= i2 + N_BACK)
    i1 = lax.broadcasted_iota(jnp.int32, (qb, qb), 0)
    j1 = lax.broadcasted_iota(jnp.int32, (qb, qb), 1)
    tril = j1 <= i1

    def rows(start, d):
        return pl.ds(start, qb, stride=d) if d > 1 else pl.ds(pl.multiple_of(start, qb), qb)

    n_br = len(DILATIONS)
    for bi, d in enumerate(DILATIONS):
        n_blocks = seq // d // qb

        def sub_blocks(i, _, bi=bi, d=d, n_blocks=n_blocks):
            chains = [(h, i * DIL_UNROLL + u) for h in range(DIL_HEADS) for u in range(DIL_UNROLL)]
            cur, q, k, v, valid = [], [], [], [], []
            for h, g in chains:
                r = g // n_blocks
                n = g % n_blocks
                c0 = r + n * (qb * d)
                cur.append(c0)
                q.append(q_refs[h][rows(c0, d), :].astype(BF16))
                if n_blocks > 1:
                    p0 = r + jnp.maximum(n - 1, 0) * (qb * d)
                    k.append(jnp.concatenate([k_refs[h][rows(p0, d), :], k_refs[h][rows(c0, d), :]], axis=0))
                    v.append(jnp.concatenate([v_refs[h][rows(p0, d), :], v_refs[h][rows(c0, d), :]], axis=0))
                    valid.append(band & (j2 >= jnp.where(n > 0, 0, qb)))
                else:
                    k.append(k_refs[h][rows(c0, d), :])
                    v.append(v_refs[h][rows(c0, d), :])
                    valid.append(tril)
            idx = range(len(chains))
            s = [lax.dot_general(q[c], k[c].astype(BF16), nt, preferred_element_type=F32) for c in idx]
            s = [jnp.where(valid[c], s[c], -jnp.inf) for c in idx]
            m = [jnp.max(s[c], axis=1, keepdims=True) for c in idx]
            p = [jnp.exp(s[c] - m[c]) for c in idx]
            den = [jnp.sum(p[c], axis=1, keepdims=True) for c in idx]
            acc = [jnp.dot(p[c].astype(BF16), v[c].astype(BF16), preferred_element_type=F32) for c in idx]
            for c, (h, _) in enumerate(chains):
                out_s[h * n_br + bi, rows(cur[c], d), :] = acc[c] * (1.0 / den[c])
                lse_s[h * n_br + bi, rows(cur[c], d), :] = jnp.broadcast_to(m[c] + jnp.log(den[c]), (qb, HEAD_DIM))
            return 0

        lax.fori_loop(0, seq // qb // DIL_UNROLL, sub_blocks, 0)

    def merge(t, _):
        rs = pl.ds(pl.multiple_of(t * qb, qb), qb)
        for h in range(DIL_HEADS):
            hs = slice(h * HEAD_DIM, (h + 1) * HEAD_DIM)
            lse = [lse_s[h * n_br + bi, rs, :] for bi in range(n_br)]
            mx = functools.reduce(jnp.maximum, lse)
            w = [jnp.exp(l - mx) for l in lse]
            num = sum(w[bi] * out_s[h * n_br + bi, rs, :] for bi in range(n_br))
            o_ref[rs, hs] = _rms(num / sum(w), g_ref[:, hs]).astype(o_ref.dtype)
        return 0

    lax.fori_loop(0, seq // qb, merge, 0)


def _dil_attention(proj, gain, batch, seq):
    width = DIL_HEADS * HEAD_DIM
    n_groups = D_DIL // width
    p3 = proj.reshape(batch, seq, proj.shape[-1])
    out = pl.pallas_call(
        _dil_kernel,
        grid=(batch, n_groups),
        in_specs=[
            pl.BlockSpec((None, seq, HEAD_DIM),
                         functools.partial(lambda b, g, *, off: (b, 0, off + g * DIL_HEADS),
                                           off=(3 * N_HEADS_SB + part * N_HEADS_DIL) + h))
            for part in range(3) for h in range(DIL_HEADS)
        ] + [pl.BlockSpec((1, width), lambda b, g: (0, g))],
        out_specs=pl.BlockSpec((None, seq, width), lambda b, g: (b, 0, g)),
        out_shape=jax.ShapeDtypeStruct((batch, seq, D_DIL), BF16),
        scratch_shapes=[pltpu.VMEM((3 * DIL_HEADS, seq, HEAD_DIM), F32)]
        + [pltpu.VMEM((DIL_HEADS * len(DILATIONS), seq, HEAD_DIM), F32)] * 2,
        compiler_params=pltpu.CompilerParams(
            dimension_semantics=("arbitrary", "arbitrary"), vmem_limit_bytes=VMEM_LIMIT),
        name="dil_attn",
    )(*([p3] * (3 * DIL_HEADS)), gain)
    return out.reshape(batch * seq, D_DIL)


def _out_kernel(sb_ref, dl_ref, w_ref, x_ref, gpost_ref, gpre_ref, x1_ref, h2_ref):
    mix = jnp.dot(sb_ref[...], w_ref[:D_SB, :], preferred_element_type=F32)
    mix += jnp.dot(dl_ref[...], w_ref[D_SB:, :], preferred_element_type=F32)
    x1 = x_ref[...] + _rms(mix, gpost_ref[...])
    x1_ref[...] = x1
    h2_ref[...] = _rms(x1, gpre_ref[...]).astype(BF16)


def _out_proj(mixed_sb, mixed_dl, w_out, x2, g_post, g_pre_ffn):
    T = x2.shape[0]
    tm = OUT_TM
    return pl.pallas_call(
        _out_kernel,
        grid=(T // tm,),
        in_specs=[
            pl.BlockSpec((tm, D_SB), lambda i: (i, 0)),
            pl.BlockSpec((tm, D_DIL), lambda i: (i, 0)),
            pl.BlockSpec((D_SB + D_DIL, D_MODEL), lambda i: (0, 0), pipeline_mode=pl.Buffered(1)),
            pl.BlockSpec((tm, D_MODEL), lambda i: (i, 0)),
            pl.BlockSpec((1, D_MODEL), lambda i: (0, 0)),
            pl.BlockSpec((1, D_MODEL), lambda i: (0, 0)),
        ],
        out_specs=[
            pl.BlockSpec((tm, D_MODEL), lambda i: (i, 0)),
            pl.BlockSpec((tm, D_MODEL), lambda i: (i, 0)),
        ],
        out_shape=[
            jax.ShapeDtypeStruct((T, D_MODEL), F32),
            jax.ShapeDtypeStruct((T, D_MODEL), BF16),
        ],
        compiler_params=pltpu.CompilerParams(
            dimension_semantics=("arbitrary",), vmem_limit_bytes=VMEM_LIMIT),
        name="out_proj",
    )(mixed_sb, mixed_dl, w_out, x2, g_post, g_pre_ffn)


def _ffn_kernel(h_ref, wg_ref, wv_ref, cwg_ref, cwv_ref, cbg_ref, cbv_ref, wdn_ref, x1_hbm, g_ref,
                o_ref, tail_ref, x1_ref, x1_sem, *, tiles_per_seq):
    i = pl.program_id(0)
    f = pl.program_id(1)
    tm, halo = FFN_TM, FFN_HALO

    @pl.when((i == 0) & (f == 0))
    def _():
        tail_ref[...] = jnp.zeros_like(tail_ref)

    def x1_copy():
        rows = pl.ds(pl.multiple_of(i * tm, tm), tm)
        return pltpu.make_async_copy(x1_hbm.at[rows, :], x1_ref, x1_sem)

    @pl.when(f == 0)
    def _():
        x1_copy().start()
        o_ref[...] = jnp.zeros_like(o_ref)

    keep = (i % tiles_per_seq) != 0

    def up_conv(part, w_ref, cw_ref, cb_ref):
        u = jnp.dot(h_ref[...], w_ref[...], preferred_element_type=F32)
        prev = jnp.where(keep, tail_ref[f, part], 0.0)
        tail_ref[f, part] = u[tm - halo:, :]
        u = jnp.concatenate([prev, u], axis=0)
        cw = cw_ref[...]
        conv = cb_ref[...] + cw[CONV_WIDTH - 1:CONV_WIDTH, :] * u[halo:, :]
        for tap in range(CONV_WIDTH - 1):
            back = CONV_WIDTH - 1 - tap
            conv += cw[tap:tap + 1, :] * u[halo - back:halo - back + tm, :]
        return conv

    gate = up_conv(0, wg_ref, cwg_ref, cbg_ref)
    value = up_conv(1, wv_ref, cwv_ref, cbv_ref)
    y = jax.nn.gelu(gate, approximate=True) * value
    o_ref[...] += jnp.dot(y.astype(BF16), wdn_ref[...], preferred_element_type=F32)

    @pl.when(f == pl.num_programs(1) - 1)
    def _():
        x1_copy().wait()
        o_ref[...] = x1_ref[...] + _rms(o_ref[...], g_ref[...])


def _conv_ffn(h2, w_up_p, conv_w_p, conv_b_p, w_down_p, x1, g_post, seq):
    T = h2.shape[0]
    tm, fc, halo = FFN_TM, FFN_FC, FFN_HALO
    n_chunks = D_FF_PAD // fc
    gate_blk = lambda i, f: (0, f)
    value_blk = lambda i, f: (0, n_chunks + f)
    return pl.pallas_call(
        functools.partial(_ffn_kernel, tiles_per_seq=seq // tm),
        grid=(T // tm, n_chunks),
        in_specs=[
            pl.BlockSpec((tm, D_MODEL), lambda i, f: (i, 0)),
            pl.BlockSpec((D_MODEL, fc), gate_blk),
            pl.BlockSpec((D_MODEL, fc), value_blk),
            pl.BlockSpec((CONV_WIDTH, fc), gate_blk),
            pl.BlockSpec((CONV_WIDTH, fc), value_blk),
            pl.BlockSpec((1, fc), gate_blk),
            pl.BlockSpec((1, fc), value_blk),
            pl.BlockSpec((fc, D_MODEL), lambda i, f: (f, 0)),
            pl.BlockSpec(memory_space=pl.ANY),
            pl.BlockSpec((1, D_MODEL), lambda i, f: (0, 0)),
        ],
        out_specs=pl.BlockSpec((tm, D_MODEL), lambda i, f: (i, 0)),
        out_shape=jax.ShapeDtypeStruct((T, D_MODEL), F32),
        scratch_shapes=[pltpu.VMEM((n_chunks, 2, halo, fc), F32),
                        pltpu.VMEM((tm, D_MODEL), F32),
                        pltpu.SemaphoreType.DMA(())],
        compiler_params=pltpu.CompilerParams(
            dimension_semantics=("arbitrary", "arbitrary"), vmem_limit_bytes=FFN_VMEM_LIMIT),
        name="conv_ffn",
    )(h2, w_up_p, w_up_p, conv_w_p, conv_w_p, conv_b_p, conv_b_p, w_down_p, x1, g_post)


def _pad_gate_value(a, dtype):
    zeros = jnp.zeros((a.shape[0], D_FF_PAD - D_FF), dtype)
    return jnp.concatenate([a[:, :D_FF].astype(dtype), zeros, a[:, D_FF:].astype(dtype), zeros], axis=1)


def _stage_up_kernel(w_ref, o_ref):
    rows = w_ref.shape[0]
    for part in range(2):
        o_ref[:, part * D_FF_PAD:part * D_FF_PAD + D_FF] = w_ref[:, part * D_FF:(part + 1) * D_FF].astype(BF16)
        o_ref[:, part * D_FF_PAD + D_FF:(part + 1) * D_FF_PAD] = jnp.zeros((rows, D_FF_PAD - D_FF), BF16)


def _stage_down_kernel(w_ref, o_ref):
    o_ref[:D_FF, :] = w_ref[...].astype(BF16)
    o_ref[D_FF:, :] = jnp.zeros((D_FF_PAD - D_FF, w_ref.shape[1]), BF16)


def _stage_ffn_weights(w_up, w_down):
    rows, cols = STAGE_UP_ROWS, STAGE_DOWN_COLS
    params = pltpu.CompilerParams(dimension_semantics=("arbitrary",), vmem_limit_bytes=VMEM_LIMIT)
    w_up_p = pl.pallas_call(
        _stage_up_kernel,
        grid=(D_MODEL // rows,),
        in_specs=[pl.BlockSpec((rows, 2 * D_FF), lambda i: (i, 0))],
        out_specs=pl.BlockSpec((rows, 2 * D_FF_PAD), lambda i: (i, 0)),
        out_shape=jax.ShapeDtypeStruct((D_MODEL, 2 * D_FF_PAD), BF16),
        compiler_params=params,
        name="stage_w_up",
    )(w_up)
    w_down_p = pl.pallas_call(
        _stage_down_kernel,
        grid=(D_MODEL // cols,),
        in_specs=[pl.BlockSpec((D_FF, cols), lambda i: (0, i))],
        out_specs=pl.BlockSpec((D_FF_PAD, cols), lambda i: (0, i)),
        out_shape=jax.ShapeDtypeStruct((D_FF_PAD, D_MODEL), BF16),
        compiler_params=params,
        name="stage_w_down",
    )(w_down)
    return w_up_p, w_down_p


def kernel(x, pre_mix_gain, post_mix_gain, pre_ffn_gain, post_ffn_gain, w_in, sb_out_gain, dil_out_gain,
           w_out, w_up, conv_w, conv_b, w_down):
    batch, seq, _ = x.shape
    depth = w_in.shape[0]
    ta, tb = _section_tables(seq)
    x2 = x.reshape(batch * seq, D_MODEL)
    for layer in range(depth):
        row = lambda g: g[layer][None, :]
        proj = _qkv_proj(x2, row(pre_mix_gain), w_in[layer].astype(BF16), ta, tb, seq)
        mixed_sb = _sb_attention(proj, row(sb_out_gain), batch, seq)
        mixed_dl = _dil_attention(proj, row(dil_out_gain), batch, seq)
        x1, h2 = _out_proj(mixed_sb, mixed_dl, w_out[layer].astype(BF16), x2,
                           row(post_mix_gain), row(pre_ffn_gain))
        w_up_p, w_down_p = _stage_ffn_weights(w_up[layer], w_down[layer])
        x2 = _conv_ffn(h2, w_up_p, _pad_gate_value(conv_w[layer], F32), _pad_gate_value(row(conv_b), F32),
                       w_down_p, x1, row(post_ffn_gain), seq)
    return x2.reshape(batch, seq, D_MODEL)
```

```python
import functools
import math

import jax
import jax.numpy as jnp
from jax import lax
from jax.experimental import pallas as pl
from jax.experimental.pallas import tpu as pltpu

F32 = jnp.float32
BF16 = jnp.bfloat16

D_MODEL = 2048
HEAD_DIM = 128
N_HEADS_SB = 8
N_HEADS_DIL = 8
D_SB = N_HEADS_SB * HEAD_DIM
D_DIL = N_HEADS_DIL * HEAD_DIM
DILATIONS = (1, 4, 16)
N_BACK = 128
D_FF = 5504
CONV_WIDTH = 3
ROPE_THETA = 10000.0
RMS_EPS = 1e-6
SCALE = HEAD_DIM ** -0.5
LOG2_E = math.log2(math.e)

LANES = 128
VMEM_LIMIT = 56 * 1024 * 1024
FFN_VMEM_LIMIT = 60 * 1024 * 1024

PROJ_TM = 1024
OUT_TM = 512
SECTION = 1024
SB_HEADS = 4
SB_TQ = 256
SB_TK = 128
SB_DEAD_LOG2 = -152.0
DIL_HEADS = 2
DIL_QB = 128
DIL_UNROLL = 4
FFN_TM = 1024
FFN_FC = 512
FFN_HALO = 8
D_FF_PAD = -(-D_FF // FFN_FC) * FFN_FC
STAGE_UP_ROWS = 128
STAGE_DOWN_COLS = 256


def _rms(x, gain):
    return x * lax.rsqrt(jnp.mean(x * x, axis=-1, keepdims=True) + RMS_EPS) * gain


def _qkv_kernel(x_ref, g_ref, w_ref, ta_ref, tb_ref, o_ref, h_ref):
    j = pl.program_id(1)

    @pl.when(j == 0)
    def _():
        h_ref[...] = _rms(x_ref[...], g_ref[...]).astype(BF16)

    acc = jnp.dot(h_ref[...], w_ref[...], preferred_element_type=F32)
    ta, tb = ta_ref[...], tb_ref[...]
    for h in range(SECTION // HEAD_DIM):
        hs = slice(h * HEAD_DIM, (h + 1) * HEAD_DIM)
        a_h = acc[:, hs]
        o_ref[:, hs] = (a_h * ta + pltpu.roll(a_h, HEAD_DIM // 2, axis=1) * tb).astype(BF16)


def _section_tables(seq):
    inv_freq = ROPE_THETA ** (-jnp.arange(0, HEAD_DIM, 2, dtype=F32) / HEAD_DIM)
    ang = jnp.arange(seq, dtype=F32)[:, None] * inv_freq[None, :]
    cos, sin = jnp.cos(ang), jnp.sin(ang)
    cos_t = jnp.concatenate([cos, cos], axis=-1)
    sin_t = jnp.concatenate([-sin, sin], axis=-1)
    one, zero = jnp.ones_like(cos_t), jnp.zeros_like(cos_t)
    ta = jnp.stack([cos_t * SCALE, cos_t, one * (SCALE * LOG2_E), one])
    tb = jnp.stack([sin_t * SCALE, sin_t, zero, zero])
    return ta, tb


def _qkv_proj(x2, gain, w_in, ta, tb, seq):
    T = x2.shape[0]
    tm = PROJ_TM
    n_pos_blocks = seq // tm

    def table_blk(i, j):
        kind = jnp.where(j == 3, 0, jnp.where(j == 4, 1, jnp.where(j == 0, 2, 3)))
        return (kind, i % n_pos_blocks, 0)

    return pl.pallas_call(
        _qkv_kernel,
        grid=(T // tm, 6),
        in_specs=[
            pl.BlockSpec((tm, D_MODEL), lambda i, j: (i, 0)),
            pl.BlockSpec((1, D_MODEL), lambda i, j: (0, 0)),
            pl.BlockSpec((D_MODEL, SECTION), lambda i, j: (0, j)),
            pl.BlockSpec((None, tm, HEAD_DIM), table_blk),
            pl.BlockSpec((None, tm, HEAD_DIM), table_blk),
        ],
        out_specs=pl.BlockSpec((tm, SECTION), lambda i, j: (i, j)),
        out_shape=jax.ShapeDtypeStruct((T, 3 * (D_SB + D_DIL)), BF16),
        scratch_shapes=[pltpu.VMEM((tm, D_MODEL), BF16)],
        compiler_params=pltpu.CompilerParams(
            dimension_semantics=("arbitrary", "arbitrary"), vmem_limit_bytes=VMEM_LIMIT),
        name="qkv_proj",
    )(x2, gain, w_in, ta, tb)


def _sb_kernel(q_ref, k_ref, v_ref, g_ref, uu_ref, o_ref, c_ref, acc_ref):
    seq = q_ref.shape[0]
    tq, tk = SB_TQ, SB_TK
    nt = (((1,), (1,)), ((), ()))
    heads = range(SB_HEADS)
    hs = [slice(h * HEAD_DIM, (h + 1) * HEAD_DIM) for h in heads]

    def aligned(row):
        return row if isinstance(row, int) else pl.multiple_of(row, tq)

    def key_tile(qb, k_start, width, ends_on_diagonal):
        n_sub = width // tk
        q0 = aligned(qb * tq)
        k0 = aligned(k_start)
        z = [lax.dot_general(q_ref[pl.ds(q0, tq), hs[h]], k_ref[pl.ds(k0, width), hs[h]], nt,
                             preferred_element_type=F32) for h in heads]
        log_beta = [jnp.minimum(z[h], 0.0) - jnp.log2(1.0 + jnp.exp2(-jnp.abs(z[h]))) for h in heads]
        log_keep = [log_beta[h] - z[h] for h in heads]
        masked = ends_on_diagonal
        if masked:
            causal = (lax.broadcasted_iota(jnp.int32, (tq, width), 0) + (width - tq)
                      > lax.broadcasted_iota(jnp.int32, (tq, width), 1))
            log_keep = [jnp.where(causal, lk, 0.0) for lk in log_keep]
        hi = [lk.astype(BF16) for lk in log_keep]
        lo = [(log_keep[h] - hi[h].astype(F32)).astype(BF16) for h in heads]
        r = [[jnp.dot(jnp.concatenate([hi[h][:, j * tk:(j + 1) * tk], lo[h][:, j * tk:(j + 1) * tk]], axis=1),
                      uu_ref[...], preferred_element_type=F32) for j in range(n_sub)] for h in heads]
        pv = []
        for h in heads:
            c = c_ref[h]
            a = [None] * n_sub
            for j in range(n_sub - 1, -1, -1):
                a[j] = jnp.exp2(log_beta[h][:, j * tk:(j + 1) * tk] + r[h][j][:, :tk] + c)
                c = c + r[h][j][:, tk:]
            c_ref[h] = c
            a = jnp.concatenate(a, axis=1)
            if masked:
                a = jnp.where(causal, a, 0.0)
            pv.append(jnp.dot(a.astype(BF16), v_ref[pl.ds(k0, width), hs[h]], preferred_element_type=F32))
        for h in heads:
            acc_ref[h] += pv[h]

    def c_max():
        return jnp.max(functools.reduce(jnp.maximum, [c_ref[h] for h in heads]))

    def finish(qb):
        q0 = aligned(qb * tq)
        for h in heads:
            o_ref[pl.ds(q0, tq), hs[h]] = _rms(acc_ref[h], g_ref[:, hs[h]]).astype(o_ref.dtype)

    def q_tile(qb, _):
        c_ref[...] = jnp.zeros_like(c_ref)
        acc_ref[...] = jnp.zeros_like(acc_ref)
        key_tile(qb, (qb - 1) * tq, 2 * tq, True)

        def live(carry):
            kt, c_hi = carry
            return (kt >= 0) & (c_hi > SB_DEAD_LOG2)

        def earlier(carry):
            kt, _ = carry
            key_tile(qb, kt * tq, tq, False)
            return kt - 1, c_max()

        lax.while_loop(live, earlier, (qb - 2, c_max()))
        finish(qb)
        return 0

    c_ref[...] = jnp.zeros_like(c_ref)
    acc_ref[...] = jnp.zeros_like(acc_ref)
    key_tile(0, 0, tq, True)
    finish(0)
    lax.fori_loop(1, seq // tq, q_tile, 0)


def _sb_attention(proj, gain, batch, seq):
    width = SB_HEADS * HEAD_DIM
    n_groups = D_SB // width
    p3 = proj.reshape(batch, seq, proj.shape[-1])
    tri = (jnp.arange(SB_TK)[:, None] > jnp.arange(SB_TK)[None, :])
    u = jnp.concatenate([tri.astype(BF16), jnp.ones((SB_TK, SB_TK), BF16)], axis=1)
    uu = jnp.concatenate([u, u], axis=0)
    out = pl.pallas_call(
        _sb_kernel,
        grid=(batch, n_groups),
        in_specs=[
            pl.BlockSpec((None, seq, width), lambda b, g: (b, 0, g)),
            pl.BlockSpec((None, seq, width), lambda b, g: (b, 0, n_groups + g)),
            pl.BlockSpec((None, seq, width), lambda b, g: (b, 0, 2 * n_groups + g)),
            pl.BlockSpec((1, width), lambda b, g: (0, g)),
            pl.BlockSpec((2 * SB_TK, 2 * SB_TK), lambda b, g: (0, 0)),
        ],
        out_specs=pl.BlockSpec((None, seq, width), lambda b, g: (b, 0, g)),
        out_shape=jax.ShapeDtypeStruct((batch, seq, D_SB), BF16),
        scratch_shapes=[pltpu.VMEM((SB_HEADS, SB_TQ, SB_TK), F32),
                        pltpu.VMEM((SB_HEADS, SB_TQ, HEAD_DIM), F32)],
        compiler_params=pltpu.CompilerParams(
            dimension_semantics=("arbitrary", "arbitrary"), vmem_limit_bytes=VMEM_LIMIT),
        name="sb_attn",
    )(p3, p3, p3, gain, uu)
    return out.reshape(batch * seq, D_SB)


def _dil_kernel(*refs):
    g_ref, o_ref, qkv_s, out_s, lse_s = refs[3 * DIL_HEADS:]
    seq = o_ref.shape[0]
    for i in range(3 * DIL_HEADS):
        qkv_s[i] = refs[i][...].astype(F32)
    q_refs, k_refs, v_refs = ([qkv_s.at[p * DIL_HEADS + h] for h in range(DIL_HEADS)] for p in range(3))
    qb = DIL_QB
    nt = (((1,), (1,)), ((), ()))
    i2 = lax.broadcasted_iota(jnp.int32, (qb, 2 * qb), 0)
    j2 = lax.broadcasted_iota(jnp.int32, (qb, 2 * qb), 1)
    band = (j2 >= i2) & (j2 <= i2 + N_BACK)
    i1 = lax.broadcasted_iota(jnp.int32, (qb, qb), 0)
    j1 = lax.broadcasted_iota(jnp.int32, (qb, qb), 1)
    tril = j1 <= i1

    def rows(start, d):
        return pl.ds(start, qb, stride=d) if d > 1 else pl.ds(pl.multiple_of(start, qb), qb)

    def block_results(i, d):
        n_blocks = seq // d // qb
        chains = [(h, i * DIL_UNROLL + u) for h in range(DIL_HEADS) for u in range(DIL_UNROLL)]
        cur, q, k, v, valid = [], [], [], [], []
        for h, g in chains:
            r = g // n_blocks
            n = g % n_blocks
            c0 = r + n * (qb * d)
            cur.append(c0)
            q.append(q_refs[h][rows(c0, d), :].astype(BF16))
            if n_blocks > 1:
                p0 = r + jnp.maximum(n - 1, 0) * (qb * d)
                k.append(jnp.concatenate([k_refs[h][rows(p0, d), :], k_refs[h][rows(c0, d), :]], axis=0))
                v.append(jnp.concatenate([v_refs[h][rows(p0, d), :], v_refs[h][rows(c0, d), :]], axis=0))
                valid.append(band & (j2 >= jnp.where(n > 0, 0, qb)))
            else:
                k.append(k_refs[h][rows(c0, d), :])
                v.append(v_refs[h][rows(c0, d), :])
                valid.append(tril)
        idx = range(len(chains))
        s = [lax.dot_general(q[c], k[c].astype(BF16), nt, preferred_element_type=F32) for c in idx]
        s = [jnp.where(valid[c], s[c], -jnp.inf) for c in idx]
        m = [jnp.max(s[c], axis=1, keepdims=True) for c in idx]
        p = [jnp.exp(s[c] - m[c]) for c in idx]
        den = [jnp.sum(p[c], axis=1, keepdims=True) for c in idx]
        acc = [jnp.dot(p[c].astype(BF16), v[c].astype(BF16), preferred_element_type=F32) for c in idx]
        return chains, cur, [acc[c] * (1.0 / den[c]) for c in idx], [m[c] + jnp.log(den[c]) for c in idx]

    strided = [d for d in DILATIONS if d > 1]
    for slot, d in enumerate(strided):
        def sub_blocks(i, _, slot=slot, d=d):
            chains, cur, outs, lses = block_results(i, d)
            for c, (h, _) in enumerate(chains):
                out_s[h * len(strided) + slot, rows(cur[c], d), :] = outs[c]
                lse_s[h * len(strided) + slot, rows(cur[c], d), :] = jnp.broadcast_to(lses[c], (qb, HEAD_DIM))
            return 0

        lax.fori_loop(0, seq // qb // DIL_UNROLL, sub_blocks, 0)

    def merge(i, _):
        chains, cur, outs, lses = block_results(i, 1)
        for c, (h, _) in enumerate(chains):
            rs = rows(cur[c], 1)
            hs = slice(h * HEAD_DIM, (h + 1) * HEAD_DIM)
            lse = [jnp.broadcast_to(lses[c], (qb, HEAD_DIM))]
            lse += [lse_s[h * len(strided) + slot, rs, :] for slot in range(len(strided))]
            branch = [outs[c]] + [out_s[h * len(strided) + slot, rs, :] for slot in range(len(strided))]
            mx = functools.reduce(jnp.maximum, lse)
            w = [jnp.exp(l - mx) for l in lse]
            num = sum(w_b * o_b for w_b, o_b in zip(w, branch))
            o_ref[rs, hs] = _rms(num / sum(w), g_ref[:, hs]).astype(o_ref.dtype)
        return 0

    lax.fori_loop(0, seq // qb // DIL_UNROLL, merge, 0)


def _dil_attention(proj, gain, batch, seq):
    width = DIL_HEADS * HEAD_DIM
    n_groups = D_DIL // width
    p3 = proj.reshape(batch, seq, proj.shape[-1])
    out = pl.pallas_call(
        _dil_kernel,
        grid=(batch, n_groups),
        in_specs=[
            pl.BlockSpec((None, seq, HEAD_DIM),
                         functools.partial(lambda b, g, *, off: (b, 0, off + g * DIL_HEADS),
                                           off=(3 * N_HEADS_SB + part * N_HEADS_DIL) + h))
            for part in range(3) for h in range(DIL_HEADS)
        ] + [pl.BlockSpec((1, width), lambda b, g: (0, g))],
        out_specs=pl.BlockSpec((None, seq, width), lambda b, g: (b, 0, g)),
        out_shape=jax.ShapeDtypeStruct((batch, seq, D_DIL), BF16),
        scratch_shapes=[pltpu.VMEM((3 * DIL_HEADS, seq, HEAD_DIM), F32)]
        + [pltpu.VMEM((DIL_HEADS * (len(DILATIONS) - 1), seq, HEAD_DIM), F32)] * 2,
        compiler_params=pltpu.CompilerParams(
            dimension_semantics=("arbitrary", "arbitrary"), vmem_limit_bytes=VMEM_LIMIT),
        name="dil_attn",
    )(*([p3] * (3 * DIL_HEADS)), gain)
    return out.reshape(batch * seq, D_DIL)


def _out_kernel(sb_ref, dl_ref, w_ref, x_ref, gpost_ref, gpre_ref, x1_ref, h2_ref):
    mix = jnp.dot(sb_ref[...], w_ref[:D_SB, :], preferred_element_type=F32)
    mix += jnp.dot(dl_ref[...], w_ref[D_SB:, :], preferred_element_type=F32)
    x1 = x_ref[...] + _rms(mix, gpost_ref[...])
    x1_ref[...] = x1
    h2_ref[...] = _rms(x1, gpre_ref[...]).astype(BF16)


def _out_proj(mixed_sb, mixed_dl, w_out, x2, g_post, g_pre_ffn):
    T = x2.shape[0]
    tm = OUT_TM
    return pl.pallas_call(
        _out_kernel,
        grid=(T // tm,),
        in_specs=[
            pl.BlockSpec((tm, D_SB), lambda i: (i, 0)),
            pl.BlockSpec((tm, D_DIL), lambda i: (i, 0)),
            pl.BlockSpec((D_SB + D_DIL, D_MODEL), lambda i: (0, 0), pipeline_mode=pl.Buffered(1)),
            pl.BlockSpec((tm, D_MODEL), lambda i: (i, 0)),
            pl.BlockSpec((1, D_MODEL), lambda i: (0, 0)),
            pl.BlockSpec((1, D_MODEL), lambda i: (0, 0)),
        ],
        out_specs=[
            pl.BlockSpec((tm, D_MODEL), lambda i: (i, 0)),
            pl.BlockSpec((tm, D_MODEL), lambda i: (i, 0)),
        ],
        out_shape=[
            jax.ShapeDtypeStruct((T, D_MODEL), F32),
            jax.ShapeDtypeStruct((T, D_MODEL), BF16),
        ],
        compiler_params=pltpu.CompilerParams(
            dimension_semantics=("arbitrary",), vmem_limit_bytes=VMEM_LIMIT),
        name="out_proj",
    )(mixed_sb, mixed_dl, w_out, x2, g_post, g_pre_ffn)


def _ffn_kernel(h_ref, wg_ref, wv_ref, cwg_ref, cwv_ref, cbg_ref, cbv_ref, wdn_ref, x1_hbm, g_ref,
                o_ref, tail_ref, x1_ref, x1_sem, *, tiles_per_seq):
    i = pl.program_id(0)
    f = pl.program_id(1)
    tm, halo = FFN_TM, FFN_HALO

    @pl.when((i == 0) & (f == 0))
    def _():
        tail_ref[...] = jnp.zeros_like(tail_ref)

    def x1_copy():
        rows = pl.ds(pl.multiple_of(i * tm, tm), tm)
        return pltpu.make_async_copy(x1_hbm.at[rows, :], x1_ref, x1_sem)

    @pl.when(f == 0)
    def _():
        x1_copy().start()
        o_ref[...] = jnp.zeros_like(o_ref)

    keep = (i % tiles_per_seq) != 0

    def up_conv(part, w_ref, cw_ref, cb_ref):
        u = jnp.dot(h_ref[...], w_ref[...], preferred_element_type=F32)
        prev = jnp.where(keep, tail_ref[f, part], 0.0)
        tail_ref[f, part] = u[tm - halo:, :]
        u = jnp.concatenate([prev, u], axis=0)
        cw = cw_ref[...]
        conv = cb_ref[...] + cw[CONV_WIDTH - 1:CONV_WIDTH, :] * u[halo:, :]
        for tap in range(CONV_WIDTH - 1):
            back = CONV_WIDTH - 1 - tap
            conv += cw[tap:tap + 1, :] * u[halo - back:halo - back + tm, :]
        return conv

    gate = up_conv(0, wg_ref, cwg_ref, cbg_ref)
    value = up_conv(1, wv_ref, cwv_ref, cbv_ref)
    y = jax.nn.gelu(gate, approximate=True) * value
    o_ref[...] += jnp.dot(y.astype(BF16), wdn_ref[...], preferred_element_type=F32)

    @pl.when(f == pl.num_programs(1) - 1)
    def _():
        x1_copy().wait()
        o_ref[...] = x1_ref[...] + _rms(o_ref[...], g_ref[...])


def _conv_ffn(h2, w_up_p, conv_w_p, conv_b_p, w_down_p, x1, g_post, seq):
    T = h2.shape[0]
    tm, fc, halo = FFN_TM, FFN_FC, FFN_HALO
    n_chunks = D_FF_PAD // fc
    gate_blk = lambda i, f: (0, f)
    value_blk = lambda i, f: (0, n_chunks + f)
    return pl.pallas_call(
        functools.partial(_ffn_kernel, tiles_per_seq=seq // tm),
        grid=(T // tm, n_chunks),
        in_specs=[
            pl.BlockSpec((tm, D_MODEL), lambda i, f: (i, 0)),
            pl.BlockSpec((D_MODEL, fc), gate_blk),
            pl.BlockSpec((D_MODEL, fc), value_blk),
            pl.BlockSpec((CONV_WIDTH, fc), gate_blk),
            pl.BlockSpec((CONV_WIDTH, fc), value_blk),
            pl.BlockSpec((1, fc), gate_blk),
            pl.BlockSpec((1, fc), value_blk),
            pl.BlockSpec((fc, D_MODEL), lambda i, f: (f, 0)),
            pl.BlockSpec(memory_space=pl.ANY),
            pl.BlockSpec((1, D_MODEL), lambda i, f: (0, 0)),
        ],
        out_specs=pl.BlockSpec((tm, D_MODEL), lambda i, f: (i, 0)),
        out_shape=jax.ShapeDtypeStruct((T, D_MODEL), F32),
        scratch_shapes=[pltpu.VMEM((n_chunks, 2, halo, fc), F32),
                        pltpu.VMEM((tm, D_MODEL), F32),
                        pltpu.SemaphoreType.DMA(())],
        compiler_params=pltpu.CompilerParams(
            dimension_semantics=("arbitrary", "arbitrary"), vmem_limit_bytes=FFN_VMEM_LIMIT),
        name="conv_ffn",
    )(h2, w_up_p, w_up_p, conv_w_p, conv_w_p, conv_b_p, conv_b_p, w_down_p, x1, g_post)


def _pad_gate_value(a, dtype):
    zeros = jnp.zeros((a.shape[0], D_FF_PAD - D_FF), dtype)
    return jnp.concatenate([a[:, :D_FF].astype(dtype), zeros, a[:, D_FF:].astype(dtype), zeros], axis=1)


def _stage_up_kernel(w_ref, o_ref):
    rows = w_ref.shape[0]
    for part in range(2):
        o_ref[:, part * D_FF_PAD:part * D_FF_PAD + D_FF] = w_ref[:, part * D_FF:(part + 1) * D_FF].astype(BF16)
        o_ref[:, part * D_FF_PAD + D_FF:(part + 1) * D_FF_PAD] = jnp.zeros((rows, D_FF_PAD - D_FF), BF16)


def _stage_down_kernel(w_ref, o_ref):
    o_ref[:D_FF, :] = w_ref[...].astype(BF16)
    o_ref[D_FF:, :] = jnp.zeros((D_FF_PAD - D_FF, w_ref.shape[1]), BF16)


def _stage_ffn_weights(w_up, w_down):
    rows, cols = STAGE_UP_ROWS, STAGE_DOWN_COLS
    params = pltpu.CompilerParams(dimension_semantics=("arbitrary",), vmem_limit_bytes=VMEM_LIMIT)
    w_up_p = pl.pallas_call(
        _stage_up_kernel,
        grid=(D_MODEL // rows,),
        in_specs=[pl.BlockSpec((rows, 2 * D_FF), lambda i: (i, 0))],
        out_specs=pl.BlockSpec((rows, 2 * D_FF_PAD), lambda i: (i, 0)),
        out_shape=jax.ShapeDtypeStruct((D_MODEL, 2 * D_FF_PAD), BF16),
        compiler_params=params,
        name="stage_w_up",
    )(w_up)
    w_down_p = pl.pallas_call(
        _stage_down_kernel,
        grid=(D_MODEL // cols,),
        in_specs=[pl.BlockSpec((D_FF, cols), lambda i: (0, i))],
        out_specs=pl.BlockSpec((D_FF_PAD, cols), lambda i: (0, i)),
        out_shape=jax.ShapeDtypeStruct((D_FF_PAD, D_MODEL), BF16),
        compiler_params=params,
        name="stage_w_down",
    )(w_down)
    return w_up_p, w_down_p


def kernel(x, pre_mix_gain, post_mix_gain, pre_ffn_gain, post_ffn_gain, w_in, sb_out_gain, dil_out_gain,
           w_out, w_up, conv_w, conv_b, w_down):
    batch, seq, _ = x.shape
    depth = w_in.shape[0]
    ta, tb = _section_tables(seq)
    x2 = x.reshape(batch * seq, D_MODEL)
    for layer in range(depth):
        row = lambda g: g[layer][None, :]
        proj = _qkv_proj(x2, row(pre_mix_gain), w_in[layer].astype(BF16), ta, tb, seq)
        mixed_sb = _sb_attention(proj, row(sb_out_gain), batch, seq)
        mixed_dl = _dil_attention(proj, row(dil_out_gain), batch, seq)
        x1, h2 = _out_proj(mixed_sb, mixed_dl, w_out[layer].astype(BF16), x2,
                           row(post_mix_gain), row(pre_ffn_gain))
        w_up_p, w_down_p = _stage_ffn_weights(w_up[layer], w_down[layer])
        x2 = _conv_ffn(h2, w_up_p, _pad_gate_value(conv_w[layer], F32), _pad_gate_value(row(conv_b), F32),
                       w_down_p, x1, row(post_ffn_gain), seq)
    return x2.reshape(batch, seq, D_MODEL)
```

```python
import functools
import math

import jax
import jax.numpy as jnp
from jax import lax
from jax.experimental import pallas as pl
from jax.experimental.pallas import tpu as pltpu

F32 = jnp.float32
BF16 = jnp.bfloat16

D_MODEL = 2048
HEAD_DIM = 128
N_HEADS_SB = 8
N_HEADS_DIL = 8
D_SB = N_HEADS_SB * HEAD_DIM
D_DIL = N_HEADS_DIL * HEAD_DIM
DILATIONS = (1, 4, 16)
N_BACK = 128
D_FF = 5504
CONV_WIDTH = 3
ROPE_THETA = 10000.0
RMS_EPS = 1e-6
SCALE = HEAD_DIM ** -0.5
LOG2_E = math.log2(math.e)

V7X_VMEM_BYTES = 64 * 1024 * 1024
VMEM_LIMIT = V7X_VMEM_BYTES - 8 * 1024 * 1024
FFN_VMEM_LIMIT = V7X_VMEM_BYTES - 4 * 1024 * 1024

PROJ_TM = 1024
OUT_TM = 512
SECTION = 1024
SB_HEADS = 4
SB_TQ = 256
SB_TK = 128
SB_DEAD_LOG2 = -152.0
DIL_HEADS = 2
DIL_QB = 128
DIL_UNROLL = 4
FFN_TM = 1024
FFN_FC = 512
FFN_HALO = 8
D_FF_PAD = -(-D_FF // FFN_FC) * FFN_FC
STAGE_UP_ROWS = 128
STAGE_DOWN_COLS = 256


def _rms(x, gain):
    return x * lax.rsqrt(jnp.mean(x * x, axis=-1, keepdims=True) + RMS_EPS) * gain


def _qkv_kernel(x_ref, g_ref, w_ref, ta_ref, tb_ref, o_ref, h_ref):
    j = pl.program_id(1)

    @pl.when(j == 0)
    def _():
        h_ref[...] = _rms(x_ref[...], g_ref[...]).astype(BF16)

    acc = jnp.dot(h_ref[...], w_ref[...], preferred_element_type=F32)
    ta, tb = ta_ref[...], tb_ref[...]
    for h in range(SECTION // HEAD_DIM):
        hs = slice(h * HEAD_DIM, (h + 1) * HEAD_DIM)
        a_h = acc[:, hs]
        o_ref[:, hs] = (a_h * ta + pltpu.roll(a_h, HEAD_DIM // 2, axis=1) * tb).astype(BF16)


def _section_tables(seq):
    inv_freq = ROPE_THETA ** (-jnp.arange(0, HEAD_DIM, 2, dtype=F32) / HEAD_DIM)
    ang = jnp.arange(seq, dtype=F32)[:, None] * inv_freq[None, :]
    cos, sin = jnp.cos(ang), jnp.sin(ang)
    cos_t = jnp.concatenate([cos, cos], axis=-1)
    sin_t = jnp.concatenate([-sin, sin], axis=-1)
    one, zero = jnp.ones_like(cos_t), jnp.zeros_like(cos_t)
    ta = jnp.stack([cos_t * SCALE, cos_t, one * (SCALE * LOG2_E), one])
    tb = jnp.stack([sin_t * SCALE, sin_t, zero, zero])
    return ta, tb


def _qkv_proj(x2, gain, w_in, ta, tb, seq):
    T = x2.shape[0]
    tm = PROJ_TM
    n_pos_blocks = seq // tm

    def table_blk(i, j):
        kind = jnp.where(j == 3, 0, jnp.where(j == 4, 1, jnp.where(j == 0, 2, 3)))
        return (kind, i % n_pos_blocks, 0)

    return pl.pallas_call(
        _qkv_kernel,
        grid=(T // tm, 6),
        in_specs=[
            pl.BlockSpec((tm, D_MODEL), lambda i, j: (i, 0)),
            pl.BlockSpec((1, D_MODEL), lambda i, j: (0, 0)),
            pl.BlockSpec((D_MODEL, SECTION), lambda i, j: (0, j)),
            pl.BlockSpec((None, tm, HEAD_DIM), table_blk),
            pl.BlockSpec((None, tm, HEAD_DIM), table_blk),
        ],
        out_specs=pl.BlockSpec((tm, SECTION), lambda i, j: (i, j)),
        out_shape=jax.ShapeDtypeStruct((T, 3 * (D_SB + D_DIL)), BF16),
        scratch_shapes=[pltpu.VMEM((tm, D_MODEL), BF16)],
        compiler_params=pltpu.CompilerParams(
            dimension_semantics=("arbitrary", "arbitrary"), vmem_limit_bytes=VMEM_LIMIT),
        name="qkv_proj",
    )(x2, gain, w_in, ta, tb)


def _sb_kernel(q_ref, k_ref, v_ref, g_ref, uu_ref, o_ref, c_ref, acc_ref):
    seq = q_ref.shape[0]
    tq, tk = SB_TQ, SB_TK
    nt = (((1,), (1,)), ((), ()))
    heads = range(SB_HEADS)
    hs = [slice(h * HEAD_DIM, (h + 1) * HEAD_DIM) for h in heads]

    def aligned(row):
        return row if isinstance(row, int) else pl.multiple_of(row, tq)

    def key_tile(qb, k_start, width, ends_on_diagonal):
        n_sub = width // tk
        q0 = aligned(qb * tq)
        k0 = aligned(k_start)
        z = [lax.dot_general(q_ref[pl.ds(q0, tq), hs[h]], k_ref[pl.ds(k0, width), hs[h]], nt,
                             preferred_element_type=F32) for h in heads]
        log_beta = [jnp.minimum(z[h], 0.0) - jnp.log2(1.0 + jnp.exp2(-jnp.abs(z[h]))) for h in heads]
        log_keep = [log_beta[h] - z[h] for h in heads]
        masked = ends_on_diagonal
        if masked:
            causal = (lax.broadcasted_iota(jnp.int32, (tq, width), 0) + (width - tq)
                      > lax.broadcasted_iota(jnp.int32, (tq, width), 1))
            log_keep = [jnp.where(causal, lk, 0.0) for lk in log_keep]
        hi = [lk.astype(BF16) for lk in log_keep]
        lo = [(log_keep[h] - hi[h].astype(F32)).astype(BF16) for h in heads]
        r = [[jnp.dot(jnp.concatenate([hi[h][:, j * tk:(j + 1) * tk], lo[h][:, j * tk:(j + 1) * tk]], axis=1),
                      uu_ref[...], preferred_element_type=F32) for j in range(n_sub)] for h in heads]
        pv = []
        for h in heads:
            c = c_ref[h]
            a = [None] * n_sub
            for j in range(n_sub - 1, -1, -1):
                a[j] = jnp.exp2(log_beta[h][:, j * tk:(j + 1) * tk] + r[h][j][:, :tk] + c)
                c = c + r[h][j][:, tk:]
            c_ref[h] = c
            a = jnp.concatenate(a, axis=1)
            if masked:
                a = jnp.where(causal, a, 0.0)
            pv.append(jnp.dot(a.astype(BF16), v_ref[pl.ds(k0, width), hs[h]], preferred_element_type=F32))
        for h in heads:
            acc_ref[h] += pv[h]

    def c_max():
        return jnp.max(functools.reduce(jnp.maximum, [c_ref[h] for h in heads]))

    def finish(qb):
        q0 = aligned(qb * tq)
        for h in heads:
            o_ref[pl.ds(q0, tq), hs[h]] = _rms(acc_ref[h], g_ref[:, hs[h]]).astype(o_ref.dtype)

    def q_tile(qb, _):
        c_ref[...] = jnp.zeros_like(c_ref)
        acc_ref[...] = jnp.zeros_like(acc_ref)
        key_tile(qb, (qb - 1) * tq, 2 * tq, True)

        def live(carry):
            kt, c_hi = carry
            return (kt >= 0) & (c_hi > SB_DEAD_LOG2)

        def earlier(carry):
            kt, _ = carry
            key_tile(qb, kt * tq, tq, False)
            return kt - 1, c_max()

        lax.while_loop(live, earlier, (qb - 2, c_max()))
        finish(qb)
        return 0

    c_ref[...] = jnp.zeros_like(c_ref)
    acc_ref[...] = jnp.zeros_like(acc_ref)
    key_tile(0, 0, tq, True)
    finish(0)
    lax.fori_loop(1, seq // tq, q_tile, 0)


def _sb_attention(proj, gain, batch, seq):
    width = SB_HEADS * HEAD_DIM
    n_groups = D_SB // width
    p3 = proj.reshape(batch, seq, proj.shape[-1])
    tri = (jnp.arange(SB_TK)[:, None] > jnp.arange(SB_TK)[None, :])
    u = jnp.concatenate([tri.astype(BF16), jnp.ones((SB_TK, SB_TK), BF16)], axis=1)
    uu = jnp.concatenate([u, u], axis=0)
    out = pl.pallas_call(
        _sb_kernel,
        grid=(batch, n_groups),
        in_specs=[
            pl.BlockSpec((None, seq, width), lambda b, g: (b, 0, g)),
            pl.BlockSpec((None, seq, width), lambda b, g: (b, 0, n_groups + g)),
            pl.BlockSpec((None, seq, width), lambda b, g: (b, 0, 2 * n_groups + g)),
            pl.BlockSpec((1, width), lambda b, g: (0, g)),
            pl.BlockSpec((2 * SB_TK, 2 * SB_TK), lambda b, g: (0, 0)),
        ],
        out_specs=pl.BlockSpec((None, seq, width), lambda b, g: (b, 0, g)),
        out_shape=jax.ShapeDtypeStruct((batch, seq, D_SB), BF16),
        scratch_shapes=[pltpu.VMEM((SB_HEADS, SB_TQ, SB_TK), F32),
                        pltpu.VMEM((SB_HEADS, SB_TQ, HEAD_DIM), F32)],
        compiler_params=pltpu.CompilerParams(
            dimension_semantics=("arbitrary", "arbitrary"), vmem_limit_bytes=VMEM_LIMIT),
        name="sb_attn",
    )(p3, p3, p3, gain, uu)
    return out.reshape(batch * seq, D_SB)


def _dil_kernel(*refs):
    g_ref, o_ref, qkv_s, out_s, lse_s = refs[3 * DIL_HEADS:]
    seq = o_ref.shape[0]
    for i in range(3 * DIL_HEADS):
        qkv_s[i] = refs[i][...].astype(F32)
    wide = [[qkv_s.at[p * DIL_HEADS + h] for h in range(DIL_HEADS)] for p in range(3)]
    narrow = [[refs[p * DIL_HEADS + h] for h in range(DIL_HEADS)] for p in range(3)]
    qb = DIL_QB
    nt = (((1,), (1,)), ((), ()))
    i2 = lax.broadcasted_iota(jnp.int32, (qb, 2 * qb), 0)
    j2 = lax.broadcasted_iota(jnp.int32, (qb, 2 * qb), 1)
    band = (j2 >= i2) & (j2 <= i2 + N_BACK)
    i1 = lax.broadcasted_iota(jnp.int32, (qb, qb), 0)
    j1 = lax.broadcasted_iota(jnp.int32, (qb, qb), 1)
    tril = j1 <= i1

    def rows(start, d):
        return pl.ds(start, qb, stride=d) if d > 1 else pl.ds(pl.multiple_of(start, qb), qb)

    def block_results(i, d):
        n_blocks = seq // d // qb
        q_refs, k_refs, v_refs = wide if d > 1 else narrow
        chains = [(h, i * DIL_UNROLL + u) for h in range(DIL_HEADS) for u in range(DIL_UNROLL)]
        cur, q, k, v, valid = [], [], [], [], []
        for h, g in chains:
            r = g // n_blocks
            n = g % n_blocks
            c0 = r + n * (qb * d)
            cur.append(c0)
            q.append(q_refs[h][rows(c0, d), :].astype(BF16))
            if n_blocks > 1:
                p0 = r + jnp.maximum(n - 1, 0) * (qb * d)
                k.append(jnp.concatenate([k_refs[h][rows(p0, d), :], k_refs[h][rows(c0, d), :]], axis=0))
                v.append(jnp.concatenate([v_refs[h][rows(p0, d), :], v_refs[h][rows(c0, d), :]], axis=0))
                valid.append(band & (j2 >= jnp.where(n > 0, 0, qb)))
            else:
                k.append(k_refs[h][rows(c0, d), :])
                v.append(v_refs[h][rows(c0, d), :])
                valid.append(tril)
        idx = range(len(chains))
        s = [lax.dot_general(q[c], k[c].astype(BF16), nt, preferred_element_type=F32) for c in idx]
        s = [jnp.where(valid[c], s[c], -jnp.inf) for c in idx]
        m = [jnp.max(s[c], axis=1, keepdims=True) for c in idx]
        p = [jnp.exp(s[c] - m[c]) for c in idx]
        den = [jnp.sum(p[c], axis=1, keepdims=True) for c in idx]
        acc = [jnp.dot(p[c].astype(BF16), v[c].astype(BF16), preferred_element_type=F32) for c in idx]
        return chains, cur, [acc[c] * (1.0 / den[c]) for c in idx], [m[c] + jnp.log(den[c]) for c in idx]

    strided = [d for d in DILATIONS if d > 1]
    for slot, d in enumerate(strided):
        def sub_blocks(i, _, slot=slot, d=d):
            chains, cur, outs, lses = block_results(i, d)
            for c, (h, _) in enumerate(chains):
                out_s[h * len(strided) + slot, rows(cur[c], d), :] = outs[c]
                lse_s[h * len(strided) + slot, rows(cur[c], d), :] = jnp.broadcast_to(lses[c], (qb, HEAD_DIM))
            return 0

        lax.fori_loop(0, seq // qb // DIL_UNROLL, sub_blocks, 0)

    def merge(i, _):
        chains, cur, outs, lses = block_results(i, 1)
        for c, (h, _) in enumerate(chains):
            rs = rows(cur[c], 1)
            hs = slice(h * HEAD_DIM, (h + 1) * HEAD_DIM)
            lse = [jnp.broadcast_to(lses[c], (qb, HEAD_DIM))]
            lse += [lse_s[h * len(strided) + slot, rs, :] for slot in range(len(strided))]
            branch = [outs[c]] + [out_s[h * len(strided) + slot, rs, :] for slot in range(len(strided))]
            mx = functools.reduce(jnp.maximum, lse)
            w = [jnp.exp(l - mx) for l in lse]
            num = sum(w_b * o_b for w_b, o_b in zip(w, branch))
            o_ref[rs, hs] = _rms(num / sum(w), g_ref[:, hs]).astype(o_ref.dtype)
        return 0

    lax.fori_loop(0, seq // qb // DIL_UNROLL, merge, 0)


def _dil_attention(proj, gain, batch, seq):
    width = DIL_HEADS * HEAD_DIM
    n_groups = D_DIL // width
    p3 = proj.reshape(batch, seq, proj.shape[-1])
    out = pl.pallas_call(
        _dil_kernel,
        grid=(batch, n_groups),
        in_specs=[
            pl.BlockSpec((None, seq, HEAD_DIM),
                         functools.partial(lambda b, g, *, off: (b, 0, off + g * DIL_HEADS),
                                           off=(3 * N_HEADS_SB + part * N_HEADS_DIL) + h))
            for part in range(3) for h in range(DIL_HEADS)
        ] + [pl.BlockSpec((1, width), lambda b, g: (0, g))],
        out_specs=pl.BlockSpec((None, seq, width), lambda b, g: (b, 0, g)),
        out_shape=jax.ShapeDtypeStruct((batch, seq, D_DIL), BF16),
        scratch_shapes=[pltpu.VMEM((3 * DIL_HEADS, seq, HEAD_DIM), F32)]
        + [pltpu.VMEM((DIL_HEADS * (len(DILATIONS) - 1), seq, HEAD_DIM), F32)] * 2,
        compiler_params=pltpu.CompilerParams(
            dimension_semantics=("arbitrary", "arbitrary"), vmem_limit_bytes=VMEM_LIMIT),
        name="dil_attn",
    )(*([p3] * (3 * DIL_HEADS)), gain)
    return out.reshape(batch * seq, D_DIL)


def _out_kernel(sb_ref, dl_ref, w_ref, x_ref, gpost_ref, gpre_ref, x1_ref, h2_ref):
    mix = jnp.dot(sb_ref[...], w_ref[:D_SB, :], preferred_element_type=F32)
    mix += jnp.dot(dl_ref[...], w_ref[D_SB:, :], preferred_element_type=F32)
    x1 = x_ref[...] + _rms(mix, gpost_ref[...])
    x1_ref[...] = x1
    h2_ref[...] = _rms(x1, gpre_ref[...]).astype(BF16)


def _out_proj(mixed_sb, mixed_dl, w_out, x2, g_post, g_pre_ffn):
    T = x2.shape[0]
    tm = OUT_TM
    return pl.pallas_call(
        _out_kernel,
        grid=(T // tm,),
        in_specs=[
            pl.BlockSpec((tm, D_SB), lambda i: (i, 0)),
            pl.BlockSpec((tm, D_DIL), lambda i: (i, 0)),
            pl.BlockSpec((D_SB + D_DIL, D_MODEL), lambda i: (0, 0), pipeline_mode=pl.Buffered(1)),
            pl.BlockSpec((tm, D_MODEL), lambda i: (i, 0)),
            pl.BlockSpec((1, D_MODEL), lambda i: (0, 0)),
            pl.BlockSpec((1, D_MODEL), lambda i: (0, 0)),
        ],
        out_specs=[
            pl.BlockSpec((tm, D_MODEL), lambda i: (i, 0)),
            pl.BlockSpec((tm, D_MODEL), lambda i: (i, 0)),
        ],
        out_shape=[
            jax.ShapeDtypeStruct((T, D_MODEL), F32),
            jax.ShapeDtypeStruct((T, D_MODEL), BF16),
        ],
        compiler_params=pltpu.CompilerParams(
            dimension_semantics=("arbitrary",), vmem_limit_bytes=VMEM_LIMIT),
        name="out_proj",
    )(mixed_sb, mixed_dl, w_out, x2, g_post, g_pre_ffn)


def _ffn_kernel(h_ref, wg_ref, wv_ref, cwg_ref, cwv_ref, cbg_ref, cbv_ref, wdn_ref, x1_hbm, g_ref,
                o_ref, tail_ref, x1_ref, x1_sem, *, tiles_per_seq):
    i = pl.program_id(0)
    f = pl.program_id(1)
    tm, halo = FFN_TM, FFN_HALO

    @pl.when((i == 0) & (f == 0))
    def _():
        tail_ref[...] = jnp.zeros_like(tail_ref)

    def x1_copy():
        rows = pl.ds(pl.multiple_of(i * tm, tm), tm)
        return pltpu.make_async_copy(x1_hbm.at[rows, :], x1_ref, x1_sem)

    @pl.when(f == 0)
    def _():
        x1_copy().start()
        o_ref[...] = jnp.zeros_like(o_ref)

    keep = (i % tiles_per_seq) != 0

    def up_conv(part, w_ref, cw_ref, cb_ref):
        u = jnp.dot(h_ref[...], w_ref[...], preferred_element_type=F32)
        prev = jnp.where(keep, tail_ref[f, part], 0.0)
        tail_ref[f, part] = u[tm - halo:, :]
        u = jnp.concatenate([prev, u], axis=0)
        cw = cw_ref[...]
        conv = cb_ref[...] + cw[CONV_WIDTH - 1:CONV_WIDTH, :] * u[halo:, :]
        for tap in range(CONV_WIDTH - 1):
            back = CONV_WIDTH - 1 - tap
            conv += cw[tap:tap + 1, :] * u[halo - back:halo - back + tm, :]
        return conv

    gate = up_conv(0, wg_ref, cwg_ref, cbg_ref)
    value = up_conv(1, wv_ref, cwv_ref, cbv_ref)
    y = jax.nn.gelu(gate, approximate=True) * value
    o_ref[...] += jnp.dot(y.astype(BF16), wdn_ref[...], preferred_element_type=F32)

    @pl.when(f == pl.num_programs(1) - 1)
    def _():
        x1_copy().wait()
        o_ref[...] = x1_ref[...] + _rms(o_ref[...], g_ref[...])


def _conv_ffn(h2, w_up_p, conv_w_p, conv_b_p, w_down_p, x1, g_post, seq):
    T = h2.shape[0]
    tm, fc, halo = FFN_TM, FFN_FC, FFN_HALO
    n_chunks = D_FF_PAD // fc
    gate_blk = lambda i, f: (0, f)
    value_blk = lambda i, f: (0, n_chunks + f)
    return pl.pallas_call(
        functools.partial(_ffn_kernel, tiles_per_seq=seq // tm),
        grid=(T // tm, n_chunks),
        in_specs=[
            pl.BlockSpec((tm, D_MODEL), lambda i, f: (i, 0)),
            pl.BlockSpec((D_MODEL, fc), gate_blk),
            pl.BlockSpec((D_MODEL, fc), value_blk),
            pl.BlockSpec((CONV_WIDTH, fc), gate_blk),
            pl.BlockSpec((CONV_WIDTH, fc), value_blk),
            pl.BlockSpec((1, fc), gate_blk),
            pl.BlockSpec((1, fc), value_blk),
            pl.BlockSpec((fc, D_MODEL), lambda i, f: (f, 0)),
            pl.BlockSpec(memory_space=pl.ANY),
            pl.BlockSpec((1, D_MODEL), lambda i, f: (0, 0)),
        ],
        out_specs=pl.BlockSpec((tm, D_MODEL), lambda i, f: (i, 0)),
        out_shape=jax.ShapeDtypeStruct((T, D_MODEL), F32),
        scratch_shapes=[pltpu.VMEM((n_chunks, 2, halo, fc), F32),
                        pltpu.VMEM((tm, D_MODEL), F32),
                        pltpu.SemaphoreType.DMA(())],
        compiler_params=pltpu.CompilerParams(
            dimension_semantics=("arbitrary", "arbitrary"), vmem_limit_bytes=FFN_VMEM_LIMIT),
        name="conv_ffn",
    )(h2, w_up_p, w_up_p, conv_w_p, conv_w_p, conv_b_p, conv_b_p, w_down_p, x1, g_post)


def _pad_gate_value(a, dtype):
    zeros = jnp.zeros((a.shape[0], D_FF_PAD - D_FF), dtype)
    return jnp.concatenate([a[:, :D_FF].astype(dtype), zeros, a[:, D_FF:].astype(dtype), zeros], axis=1)


def _stage_up_kernel(w_ref, o_ref):
    rows = w_ref.shape[0]
    for part in range(2):
        o_ref[:, part * D_FF_PAD:part * D_FF_PAD + D_FF] = w_ref[:, part * D_FF:(part + 1) * D_FF].astype(BF16)
        o_ref[:, part * D_FF_PAD + D_FF:(part + 1) * D_FF_PAD] = jnp.zeros((rows, D_FF_PAD - D_FF), BF16)


def _stage_down_kernel(w_ref, o_ref):
    o_ref[:D_FF, :] = w_ref[...].astype(BF16)
    o_ref[D_FF:, :] = jnp.zeros((D_FF_PAD - D_FF, w_ref.shape[1]), BF16)


def _stage_ffn_weights(w_up, w_down):
    rows, cols = STAGE_UP_ROWS, STAGE_DOWN_COLS
    params = pltpu.CompilerParams(dimension_semantics=("arbitrary",), vmem_limit_bytes=VMEM_LIMIT)
    w_up_p = pl.pallas_call(
        _stage_up_kernel,
        grid=(D_MODEL // rows,),
        in_specs=[pl.BlockSpec((rows, 2 * D_FF), lambda i: (i, 0))],
        out_specs=pl.BlockSpec((rows, 2 * D_FF_PAD), lambda i: (i, 0)),
        out_shape=jax.ShapeDtypeStruct((D_MODEL, 2 * D_FF_PAD), BF16),
        compiler_params=params,
        name="stage_w_up",
    )(w_up)
    w_down_p = pl.pallas_call(
        _stage_down_kernel,
        grid=(D_MODEL // cols,),
        in_specs=[pl.BlockSpec((D_FF, cols), lambda i: (0, i))],
        out_specs=pl.BlockSpec((D_FF_PAD, cols), lambda i: (0, i)),
        out_shape=jax.ShapeDtypeStruct((D_FF_PAD, D_MODEL), BF16),
        compiler_params=params,
        name="stage_w_down",
    )(w_down)
    return w_up_p, w_down_p


def kernel(x, pre_mix_gain, post_mix_gain, pre_ffn_gain, post_ffn_gain, w_in, sb_out_gain, dil_out_gain,
           w_out, w_up, conv_w, conv_b, w_down):
    batch, seq, _ = x.shape
    depth = w_in.shape[0]
    ta, tb = _section_tables(seq)
    x2 = x.reshape(batch * seq, D_MODEL)
    for layer in range(depth):
        row = lambda g: g[layer][None, :]
        proj = _qkv_proj(x2, row(pre_mix_gain), w_in[layer].astype(BF16), ta, tb, seq)
        mixed_sb = _sb_attention(proj, row(sb_out_gain), batch, seq)
        mixed_dl = _dil_attention(proj, row(dil_out_gain), batch, seq)
        x1, h2 = _out_proj(mixed_sb, mixed_dl, w_out[layer].astype(BF16), x2,
                           row(post_mix_gain), row(pre_ffn_gain))
        w_up_p, w_down_p = _stage_ffn_weights(w_up[layer], w_down[layer])
        x2 = _conv_ffn(h2, w_up_p, _pad_gate_value(conv_w[layer], F32), _pad_gate_value(row(conv_b), F32),
                       w_down_p, x1, row(post_ffn_gain), seq)
    return x2.reshape(batch, seq, D_MODEL)
```

```python
import functools
import math

import jax
import jax.numpy as jnp
from jax import lax
from jax.experimental import pallas as pl
from jax.experimental.pallas import tpu as pltpu

F32 = jnp.float32
BF16 = jnp.bfloat16

D_MODEL = 2048
HEAD_DIM = 128
N_HEADS_SB = 8
N_HEADS_DIL = 8
D_SB = N_HEADS_SB * HEAD_DIM
D_DIL = N_HEADS_DIL * HEAD_DIM
DILATIONS = (1, 4, 16)
N_BACK = 128
D_FF = 5504
CONV_WIDTH = 3
ROPE_THETA = 10000.0
RMS_EPS = 1e-6
SCALE = HEAD_DIM ** -0.5
LOG2_E = math.log2(math.e)

V7X_VMEM_BYTES = 64 * 1024 * 1024
VMEM_LIMIT = V7X_VMEM_BYTES - 8 * 1024 * 1024
FFN_VMEM_LIMIT = V7X_VMEM_BYTES - 4 * 1024 * 1024

PROJ_TM = 1024
OUT_TM = 512
SECTION = 1024
SB_HEADS = 4
SB_TQ = 256
SB_TK = 128
SB_DEAD_LOG2 = -152.0
DIL_HEADS = 2
DIL_QB = 128
DIL_UNROLL = 4
FFN_TM = 1024
FFN_FC = 512
FFN_HALO = 8
D_FF_PAD = -(-D_FF // FFN_FC) * FFN_FC
STAGE_UP_ROWS = 128
STAGE_DOWN_COLS = 256


def _rms(x, gain):
    return x * lax.rsqrt(jnp.mean(x * x, axis=-1, keepdims=True) + RMS_EPS) * gain


def _qkv_kernel(x_ref, g_ref, w_ref, ta_ref, tb_ref, o_ref, h_ref):
    j = pl.program_id(1)

    @pl.when(j == 0)
    def _():
        h_ref[...] = _rms(x_ref[...], g_ref[...]).astype(BF16)

    acc = jnp.dot(h_ref[...], w_ref[...], preferred_element_type=F32)
    ta, tb = ta_ref[...], tb_ref[...]
    for h in range(SECTION // HEAD_DIM):
        hs = slice(h * HEAD_DIM, (h + 1) * HEAD_DIM)
        a_h = acc[:, hs]
        o_ref[:, hs] = (a_h * ta + pltpu.roll(a_h, HEAD_DIM // 2, axis=1) * tb).astype(BF16)


def _section_tables(seq):
    inv_freq = ROPE_THETA ** (-jnp.arange(0, HEAD_DIM, 2, dtype=F32) / HEAD_DIM)
    ang = jnp.arange(seq, dtype=F32)[:, None] * inv_freq[None, :]
    cos, sin = jnp.cos(ang), jnp.sin(ang)
    cos_t = jnp.concatenate([cos, cos], axis=-1)
    sin_t = jnp.concatenate([-sin, sin], axis=-1)
    one, zero = jnp.ones_like(cos_t), jnp.zeros_like(cos_t)
    ta = jnp.stack([cos_t * SCALE, cos_t, one * (SCALE * LOG2_E), one])
    tb = jnp.stack([sin_t * SCALE, sin_t, zero, zero])
    return ta, tb


def _qkv_proj(x2, gain, w_in, ta, tb, seq):
    T = x2.shape[0]
    tm = PROJ_TM
    n_pos_blocks = seq // tm

    def table_blk(i, j):
        kind = jnp.where(j == 3, 0, jnp.where(j == 4, 1, jnp.where(j == 0, 2, 3)))
        return (kind, i % n_pos_blocks, 0)

    return pl.pallas_call(
        _qkv_kernel,
        grid=(T // tm, 6),
        in_specs=[
            pl.BlockSpec((tm, D_MODEL), lambda i, j: (i, 0)),
            pl.BlockSpec((1, D_MODEL), lambda i, j: (0, 0)),
            pl.BlockSpec((D_MODEL, SECTION), lambda i, j: (0, j)),
            pl.BlockSpec((None, tm, HEAD_DIM), table_blk),
            pl.BlockSpec((None, tm, HEAD_DIM), table_blk),
        ],
        out_specs=pl.BlockSpec((tm, SECTION), lambda i, j: (i, j)),
        out_shape=jax.ShapeDtypeStruct((T, 3 * (D_SB + D_DIL)), BF16),
        scratch_shapes=[pltpu.VMEM((tm, D_MODEL), BF16)],
        compiler_params=pltpu.CompilerParams(
            dimension_semantics=("arbitrary", "arbitrary"), vmem_limit_bytes=VMEM_LIMIT),
        name="qkv_proj",
    )(x2, gain, w_in, ta, tb)


def _sb_kernel(q_ref, k_ref, v_ref, g_ref, uu_ref, o_ref, c_ref, acc_ref):
    seq = q_ref.shape[0]
    tq, tk = SB_TQ, SB_TK
    nt = (((1,), (1,)), ((), ()))
    heads = range(SB_HEADS)
    hs = [slice(h * HEAD_DIM, (h + 1) * HEAD_DIM) for h in heads]

    def aligned(row):
        return row if isinstance(row, int) else pl.multiple_of(row, tq)

    def key_tile(qb, k_start, width, ends_on_diagonal):
        n_sub = width // tk
        q0 = aligned(qb * tq)
        k0 = aligned(k_start)
        z = [lax.dot_general(q_ref[pl.ds(q0, tq), hs[h]], k_ref[pl.ds(k0, width), hs[h]], nt,
                             preferred_element_type=F32) for h in heads]
        log_beta = [jnp.minimum(z[h], 0.0) - jnp.log2(1.0 + jnp.exp2(-jnp.abs(z[h]))) for h in heads]
        log_keep = [log_beta[h] - z[h] for h in heads]
        masked = ends_on_diagonal
        if masked:
            causal = (lax.broadcasted_iota(jnp.int32, (tq, width), 0) + (width - tq)
                      > lax.broadcasted_iota(jnp.int32, (tq, width), 1))
            log_keep = [jnp.where(causal, lk, 0.0) for lk in log_keep]
        hi = [lk.astype(BF16) for lk in log_keep]
        lo = [(log_keep[h] - hi[h].astype(F32)).astype(BF16) for h in heads]
        r = [[jnp.dot(jnp.concatenate([hi[h][:, j * tk:(j + 1) * tk], lo[h][:, j * tk:(j + 1) * tk]], axis=1),
                      uu_ref[...], preferred_element_type=F32) for j in range(n_sub)] for h in heads]
        pv = []
        for h in heads:
            c = c_ref[h]
            a = [None] * n_sub
            for j in range(n_sub - 1, -1, -1):
                a[j] = jnp.exp2(log_beta[h][:, j * tk:(j + 1) * tk] + r[h][j][:, :tk] + c)
                c = c + r[h][j][:, tk:]
            c_ref[h] = c
            a = jnp.concatenate(a, axis=1)
            if masked:
                a = jnp.where(causal, a, 0.0)
            pv.append(jnp.dot(a.astype(BF16), v_ref[pl.ds(k0, width), hs[h]], preferred_element_type=F32))
        for h in heads:
            acc_ref[h] += pv[h]

    def c_max():
        return jnp.max(functools.reduce(jnp.maximum, [c_ref[h] for h in heads]))

    def finish(qb):
        q0 = aligned(qb * tq)
        for h in heads:
            o_ref[pl.ds(q0, tq), hs[h]] = _rms(acc_ref[h], g_ref[:, hs[h]]).astype(o_ref.dtype)

    def q_tile(qb, _):
        c_ref[...] = jnp.zeros_like(c_ref)
        acc_ref[...] = jnp.zeros_like(acc_ref)
        key_tile(qb, (qb - 1) * tq, 2 * tq, True)

        def live(carry):
            kt, c_hi = carry
            return (kt >= 0) & (c_hi > SB_DEAD_LOG2)

        def earlier(carry):
            kt, _ = carry
            key_tile(qb, kt * tq, tq, False)
            return kt - 1, c_max()

        lax.while_loop(live, earlier, (qb - 2, c_max()))
        finish(qb)
        return 0

    c_ref[...] = jnp.zeros_like(c_ref)
    acc_ref[...] = jnp.zeros_like(acc_ref)
    key_tile(0, 0, tq, True)
    finish(0)
    lax.fori_loop(1, seq // tq, q_tile, 0)


def _sb_attention(proj, gain, batch, seq):
    width = SB_HEADS * HEAD_DIM
    n_groups = D_SB // width
    p3 = proj.reshape(batch, seq, proj.shape[-1])
    tri = (jnp.arange(SB_TK)[:, None] > jnp.arange(SB_TK)[None, :])
    u = jnp.concatenate([tri.astype(BF16), jnp.ones((SB_TK, SB_TK), BF16)], axis=1)
    uu = jnp.concatenate([u, u], axis=0)
    out = pl.pallas_call(
        _sb_kernel,
        grid=(batch, n_groups),
        in_specs=[
            pl.BlockSpec((None, seq, width), lambda b, g: (b, 0, g)),
            pl.BlockSpec((None, seq, width), lambda b, g: (b, 0, n_groups + g)),
            pl.BlockSpec((None, seq, width), lambda b, g: (b, 0, 2 * n_groups + g)),
            pl.BlockSpec((1, width), lambda b, g: (0, g)),
            pl.BlockSpec((2 * SB_TK, 2 * SB_TK), lambda b, g: (0, 0)),
        ],
        out_specs=pl.BlockSpec((None, seq, width), lambda b, g: (b, 0, g)),
        out_shape=jax.ShapeDtypeStruct((batch, seq, D_SB), BF16),
        scratch_shapes=[pltpu.VMEM((SB_HEADS, SB_TQ, SB_TK), F32),
                        pltpu.VMEM((SB_HEADS, SB_TQ, HEAD_DIM), F32)],
        compiler_params=pltpu.CompilerParams(
            dimension_semantics=("arbitrary", "arbitrary"), vmem_limit_bytes=VMEM_LIMIT),
        name="sb_attn",
    )(p3, p3, p3, gain, uu)
    return out.reshape(batch * seq, D_SB)


def _dil_kernel(*refs):
    g_ref, o_ref, qkv_s, out_s, lse_s = refs[3 * DIL_HEADS:]
    seq = o_ref.shape[0]
    for i in range(3 * DIL_HEADS):
        qkv_s[i] = refs[i][...].astype(F32)
    q_refs, k_refs, v_refs = ([qkv_s.at[p * DIL_HEADS + h] for h in range(DIL_HEADS)] for p in range(3))
    qb = DIL_QB
    nt = (((1,), (1,)), ((), ()))
    i2 = lax.broadcasted_iota(jnp.int32, (qb, 2 * qb), 0)
    j2 = lax.broadcasted_iota(jnp.int32, (qb, 2 * qb), 1)
    band = (j2 >= i2) & (j2 <= i2 + N_BACK)
    i1 = lax.broadcasted_iota(jnp.int32, (qb, qb), 0)
    j1 = lax.broadcasted_iota(jnp.int32, (qb, qb), 1)
    tril = j1 <= i1

    def rows(start, d):
        return pl.ds(start, qb, stride=d) if d > 1 else pl.ds(pl.multiple_of(start, qb), qb)

    def block_results(i, d):
        n_blocks = seq // d // qb
        chains = [(h, i * DIL_UNROLL + u) for h in range(DIL_HEADS) for u in range(DIL_UNROLL)]
        cur, q, k, v, valid = [], [], [], [], []
        for h, g in chains:
            r = g // n_blocks
            n = g % n_blocks
            c0 = r + n * (qb * d)
            cur.append(c0)
            q.append(q_refs[h][rows(c0, d), :].astype(BF16))
            if n_blocks > 1:
                p0 = r + jnp.maximum(n - 1, 0) * (qb * d)
                k.append(jnp.concatenate([k_refs[h][rows(p0, d), :], k_refs[h][rows(c0, d), :]], axis=0))
                v.append(jnp.concatenate([v_refs[h][rows(p0, d), :], v_refs[h][rows(c0, d), :]], axis=0))
                valid.append(band & (j2 >= jnp.where(n > 0, 0, qb)))
            else:
                k.append(k_refs[h][rows(c0, d), :])
                v.append(v_refs[h][rows(c0, d), :])
                valid.append(tril)
        idx = range(len(chains))
        s = [lax.dot_general(q[c], k[c].astype(BF16), nt, preferred_element_type=F32) for c in idx]
        s = [jnp.where(valid[c], s[c], -jnp.inf) for c in idx]
        m = [jnp.max(s[c], axis=1, keepdims=True) for c in idx]
        p = [jnp.exp(s[c] - m[c]) for c in idx]
        den = [jnp.sum(p[c], axis=1, keepdims=True) for c in idx]
        acc = [jnp.dot(p[c].astype(BF16), v[c].astype(BF16), preferred_element_type=F32) for c in idx]
        return chains, cur, [acc[c] * (1.0 / den[c]) for c in idx], [m[c] + jnp.log(den[c]) for c in idx]

    strided = [d for d in DILATIONS if d > 1]
    for slot, d in enumerate(strided):
        def sub_blocks(i, _, slot=slot, d=d):
            chains, cur, outs, lses = block_results(i, d)
            for c, (h, _) in enumerate(chains):
                out_s[h * len(strided) + slot, rows(cur[c], d), :] = outs[c]
                lse_s[h * len(strided) + slot, rows(cur[c], d), :] = jnp.broadcast_to(lses[c], (qb, HEAD_DIM))
            return 0

        lax.fori_loop(0, seq // qb // DIL_UNROLL, sub_blocks, 0)

    def merge(i, _):
        chains, cur, outs, lses = block_results(i, 1)
        for c, (h, _) in enumerate(chains):
            rs = rows(cur[c], 1)
            hs = slice(h * HEAD_DIM, (h + 1) * HEAD_DIM)
            lse = [jnp.broadcast_to(lses[c], (qb, HEAD_DIM))]
            lse += [lse_s[h * len(strided) + slot, rs, :] for slot in range(len(strided))]
            branch = [outs[c]] + [out_s[h * len(strided) + slot, rs, :] for slot in range(len(strided))]
            mx = functools.reduce(jnp.maximum, lse)
            w = [jnp.exp(l - mx) for l in lse]
            num = sum(w_b * o_b for w_b, o_b in zip(w, branch))
            o_ref[rs, hs] = _rms(num / sum(w), g_ref[:, hs]).astype(o_ref.dtype)
        return 0

    lax.fori_loop(0, seq // qb // DIL_UNROLL, merge, 0)


def _dil_attention(proj, gain, batch, seq):
    width = DIL_HEADS * HEAD_DIM
    n_groups = D_DIL // width
    p3 = proj.reshape(batch, seq, proj.shape[-1])
    out = pl.pallas_call(
        _dil_kernel,
        grid=(batch, n_groups),
        in_specs=[
            pl.BlockSpec((None, seq, HEAD_DIM),
                         functools.partial(lambda b, g, *, off: (b, 0, off + g * DIL_HEADS),
                                           off=(3 * N_HEADS_SB + part * N_HEADS_DIL) + h))
            for part in range(3) for h in range(DIL_HEADS)
        ] + [pl.BlockSpec((1, width), lambda b, g: (0, g))],
        out_specs=pl.BlockSpec((None, seq, width), lambda b, g: (b, 0, g)),
        out_shape=jax.ShapeDtypeStruct((batch, seq, D_DIL), BF16),
        scratch_shapes=[pltpu.VMEM((3 * DIL_HEADS, seq, HEAD_DIM), F32)]
        + [pltpu.VMEM((DIL_HEADS * (len(DILATIONS) - 1), seq, HEAD_DIM), F32)] * 2,
        compiler_params=pltpu.CompilerParams(
            dimension_semantics=("arbitrary", "arbitrary"), vmem_limit_bytes=VMEM_LIMIT),
        name="dil_attn",
    )(*([p3] * (3 * DIL_HEADS)), gain)
    return out.reshape(batch * seq, D_DIL)


def _out_kernel(sb_ref, dl_ref, w_ref, x_ref, gpost_ref, gpre_ref, x1_ref, h2_ref):
    mix = jnp.dot(sb_ref[...], w_ref[:D_SB, :], preferred_element_type=F32)
    mix += jnp.dot(dl_ref[...], w_ref[D_SB:, :], preferred_element_type=F32)
    x1 = x_ref[...] + _rms(mix, gpost_ref[...])
    x1_ref[...] = x1
    h2_ref[...] = _rms(x1, gpre_ref[...]).astype(BF16)


def _out_proj(mixed_sb, mixed_dl, w_out, x2, g_post, g_pre_ffn):
    T = x2.shape[0]
    tm = OUT_TM
    return pl.pallas_call(
        _out_kernel,
        grid=(T // tm,),
        in_specs=[
            pl.BlockSpec((tm, D_SB), lambda i: (i, 0)),
            pl.BlockSpec((tm, D_DIL), lambda i: (i, 0)),
            pl.BlockSpec((D_SB + D_DIL, D_MODEL), lambda i: (0, 0), pipeline_mode=pl.Buffered(1)),
            pl.BlockSpec((tm, D_MODEL), lambda i: (i, 0)),
            pl.BlockSpec((1, D_MODEL), lambda i: (0, 0)),
            pl.BlockSpec((1, D_MODEL), lambda i: (0, 0)),
        ],
        out_specs=[
            pl.BlockSpec((tm, D_MODEL), lambda i: (i, 0)),
            pl.BlockSpec((tm, D_MODEL), lambda i: (i, 0)),
        ],
        out_shape=[
            jax.ShapeDtypeStruct((T, D_MODEL), F32),
            jax.ShapeDtypeStruct((T, D_MODEL), BF16),
        ],
        compiler_params=pltpu.CompilerParams(
            dimension_semantics=("arbitrary",), vmem_limit_bytes=VMEM_LIMIT),
        name="out_proj",
    )(mixed_sb, mixed_dl, w_out, x2, g_post, g_pre_ffn)


def _ffn_kernel(h_ref, wg_ref, wv_ref, cwg_ref, cwv_ref, cbg_ref, cbv_ref, wdn_ref, x1_hbm, g_ref,
                o_ref, tail_ref, x1_ref, x1_sem, *, tiles_per_seq):
    i = pl.program_id(0)
    f = pl.program_id(1)
    tm, halo = FFN_TM, FFN_HALO

    @pl.when((i == 0) & (f == 0))
    def _():
        tail_ref[...] = jnp.zeros_like(tail_ref)

    def x1_copy():
        rows = pl.ds(pl.multiple_of(i * tm, tm), tm)
        return pltpu.make_async_copy(x1_hbm.at[rows, :], x1_ref, x1_sem)

    @pl.when(f == 0)
    def _():
        x1_copy().start()
        o_ref[...] = jnp.zeros_like(o_ref)

    keep = (i % tiles_per_seq) != 0

    def up_conv(part, w_ref, cw_ref, cb_ref):
        u = jnp.dot(h_ref[...], w_ref[...], preferred_element_type=F32)
        prev = jnp.where(keep, tail_ref[f, part], 0.0)
        tail_ref[f, part] = u[tm - halo:, :]
        u = jnp.concatenate([prev, u], axis=0)
        cw = cw_ref[...]
        conv = cb_ref[...] + cw[CONV_WIDTH - 1:CONV_WIDTH, :] * u[halo:, :]
        for tap in range(CONV_WIDTH - 1):
            back = CONV_WIDTH - 1 - tap
            conv += cw[tap:tap + 1, :] * u[halo - back:halo - back + tm, :]
        return conv

    gate = up_conv(0, wg_ref, cwg_ref, cbg_ref)
    value = up_conv(1, wv_ref, cwv_ref, cbv_ref)
    y = jax.nn.gelu(gate, approximate=True) * value
    o_ref[...] += jnp.dot(y.astype(BF16), wdn_ref[...], preferred_element_type=F32)

    @pl.when(f == pl.num_programs(1) - 1)
    def _():
        x1_copy().wait()
        o_ref[...] = x1_ref[...] + _rms(o_ref[...], g_ref[...])


def _conv_ffn(h2, w_up_p, conv_w_p, conv_b_p, w_down_p, x1, g_post, seq):
    T = h2.shape[0]
    tm, fc, halo = FFN_TM, FFN_FC, FFN_HALO
    n_chunks = D_FF_PAD // fc
    gate_blk = lambda i, f: (0, f)
    value_blk = lambda i, f: (0, n_chunks + f)
    return pl.pallas_call(
        functools.partial(_ffn_kernel, tiles_per_seq=seq // tm),
        grid=(T // tm, n_chunks),
        in_specs=[
            pl.BlockSpec((tm, D_MODEL), lambda i, f: (i, 0)),
            pl.BlockSpec((D_MODEL, fc), gate_blk),
            pl.BlockSpec((D_MODEL, fc), value_blk),
            pl.BlockSpec((CONV_WIDTH, fc), gate_blk),
            pl.BlockSpec((CONV_WIDTH, fc), value_blk),
            pl.BlockSpec((1, fc), gate_blk),
            pl.BlockSpec((1, fc), value_blk),
            pl.BlockSpec((fc, D_MODEL), lambda i, f: (f, 0)),
            pl.BlockSpec(memory_space=pl.ANY),
            pl.BlockSpec((1, D_MODEL), lambda i, f: (0, 0)),
        ],
        out_specs=pl.BlockSpec((tm, D_MODEL), lambda i, f: (i, 0)),
        out_shape=jax.ShapeDtypeStruct((T, D_MODEL), F32),
        scratch_shapes=[pltpu.VMEM((n_chunks, 2, halo, fc), F32),
                        pltpu.VMEM((tm, D_MODEL), F32),
                        pltpu.SemaphoreType.DMA(())],
        compiler_params=pltpu.CompilerParams(
            dimension_semantics=("arbitrary", "arbitrary"), vmem_limit_bytes=FFN_VMEM_LIMIT),
        name="conv_ffn",
    )(h2, w_up_p, w_up_p, conv_w_p, conv_w_p, conv_b_p, conv_b_p, w_down_p, x1, g_post)


def _pad_gate_value(a, dtype):
    zeros = jnp.zeros((a.shape[0], D_FF_PAD - D_FF), dtype)
    return jnp.concatenate([a[:, :D_FF].astype(dtype), zeros, a[:, D_FF:].astype(dtype), zeros], axis=1)


def _stage_up_kernel(w_ref, o_ref):
    rows = w_ref.shape[0]
    for part in range(2):
        o_ref[:, part * D_FF_PAD:part * D_FF_PAD + D_FF] = w_ref[:, part * D_FF:(part + 1) * D_FF].astype(BF16)
        o_ref[:, part * D_FF_PAD + D_FF:(part + 1) * D_FF_PAD] = jnp.zeros((rows, D_FF_PAD - D_FF), BF16)


def _stage_down_kernel(w_ref, o_ref):
    o_ref[:D_FF, :] = w_ref[...].astype(BF16)
    o_ref[D_FF:, :] = jnp.zeros((D_FF_PAD - D_FF, w_ref.shape[1]), BF16)


def _stage_ffn_weights(w_up, w_down):
    rows, cols = STAGE_UP_ROWS, STAGE_DOWN_COLS
    params = pltpu.CompilerParams(dimension_semantics=("arbitrary",), vmem_limit_bytes=VMEM_LIMIT)
    w_up_p = pl.pallas_call(
        _stage_up_kernel,
        grid=(D_MODEL // rows,),
        in_specs=[pl.BlockSpec((rows, 2 * D_FF), lambda i: (i, 0))],
        out_specs=pl.BlockSpec((rows, 2 * D_FF_PAD), lambda i: (i, 0)),
        out_shape=jax.ShapeDtypeStruct((D_MODEL, 2 * D_FF_PAD), BF16),
        compiler_params=params,
        name="stage_w_up",
    )(w_up)
    w_down_p = pl.pallas_call(
        _stage_down_kernel,
        grid=(D_MODEL // cols,),
        in_specs=[pl.BlockSpec((D_FF, cols), lambda i: (0, i))],
        out_specs=pl.BlockSpec((D_FF_PAD, cols), lambda i: (0, i)),
        out_shape=jax.ShapeDtypeStruct((D_FF_PAD, D_MODEL), BF16),
        compiler_params=params,
        name="stage_w_down",
    )(w_down)
    return w_up_p, w_down_p


def kernel(x, pre_mix_gain, post_mix_gain, pre_ffn_gain, post_ffn_gain, w_in, sb_out_gain, dil_out_gain,
           w_out, w_up, conv_w, conv_b, w_down):
    batch, seq, _ = x.shape
    depth = w_in.shape[0]
    ta, tb = _section_tables(seq)
    x2 = x.reshape(batch * seq, D_MODEL)
    for layer in range(depth):
        row = lambda g: g[layer][None, :]
        proj = _qkv_proj(x2, row(pre_mix_gain), w_in[layer].astype(BF16), ta, tb, seq)
        mixed_sb = _sb_attention(proj, row(sb_out_gain), batch, seq)
        mixed_dl = _dil_attention(proj, row(dil_out_gain), batch, seq)
        x1, h2 = _out_proj(mixed_sb, mixed_dl, w_out[layer].astype(BF16), x2,
                           row(post_mix_gain), row(pre_ffn_gain))
        w_up_p, w_down_p = _stage_ffn_weights(w_up[layer], w_down[layer])
        x2 = _conv_ffn(h2, w_up_p, _pad_gate_value(conv_w[layer], F32), _pad_gate_value(row(conv_b), F32),
                       w_down_p, x1, row(post_ffn_gain), seq)
    return x2.reshape(batch, seq, D_MODEL)
```

```python
import functools
import math

import jax
import jax.numpy as jnp
from jax import lax
from jax.experimental import pallas as pl
from jax.experimental.pallas import tpu as pltpu

F32 = jnp.float32
BF16 = jnp.bfloat16

D_MODEL = 2048
HEAD_DIM = 128
N_HEADS_SB = 8
N_HEADS_DIL = 8
D_SB = N_HEADS_SB * HEAD_DIM
D_DIL = N_HEADS_DIL * HEAD_DIM
DILATIONS = (1, 4, 16)
N_BACK = 128
D_FF = 5504
CONV_WIDTH = 3
ROPE_THETA = 10000.0
RMS_EPS = 1e-6
SCALE = HEAD_DIM ** -0.5
LOG2_E = math.log2(math.e)

V7X_VMEM_BYTES = 64 * 1024 * 1024
VMEM_LIMIT = V7X_VMEM_BYTES - 8 * 1024 * 1024
FFN_VMEM_LIMIT = V7X_VMEM_BYTES - 4 * 1024 * 1024

PROJ_TM = 1024
OUT_TM = 512
SECTION = 1024
SB_HEADS = 4
SB_TQ = 256
SB_TK = 128
SB_DEAD_LOG2 = -152.0
DIL_HEADS = 2
DIL_QB = 128
DIL_UNROLL = 4
FFN_TM = 1024
FFN_FC = 512
FFN_HALO = 8
D_FF_PAD = -(-D_FF // FFN_FC) * FFN_FC
STAGE_UP_ROWS = 128
STAGE_DOWN_COLS = 256


def _rms(x, gain):
    return x * lax.rsqrt(jnp.mean(x * x, axis=-1, keepdims=True) + RMS_EPS) * gain


def _qkv_kernel(x_ref, g_ref, w_ref, ta_ref, tb_ref, o_ref, h_ref):
    j = pl.program_id(1)

    @pl.when(j == 0)
    def _():
        h_ref[...] = _rms(x_ref[...], g_ref[...]).astype(BF16)

    acc = jnp.dot(h_ref[...], w_ref[...], preferred_element_type=F32)
    ta, tb = ta_ref[...], tb_ref[...]
    for h in range(SECTION // HEAD_DIM):
        hs = slice(h * HEAD_DIM, (h + 1) * HEAD_DIM)
        a_h = acc[:, hs]
        o_ref[:, hs] = (a_h * ta + pltpu.roll(a_h, HEAD_DIM // 2, axis=1) * tb).astype(BF16)


def _section_tables(seq):
    inv_freq = ROPE_THETA ** (-jnp.arange(0, HEAD_DIM, 2, dtype=F32) / HEAD_DIM)
    ang = jnp.arange(seq, dtype=F32)[:, None] * inv_freq[None, :]
    cos, sin = jnp.cos(ang), jnp.sin(ang)
    cos_t = jnp.concatenate([cos, cos], axis=-1)
    sin_t = jnp.concatenate([-sin, sin], axis=-1)
    one, zero = jnp.ones_like(cos_t), jnp.zeros_like(cos_t)
    ta = jnp.stack([cos_t * SCALE, cos_t, one * (SCALE * LOG2_E), one])
    tb = jnp.stack([sin_t * SCALE, sin_t, zero, zero])
    return ta, tb


def _qkv_proj(x2, gain, w_in, ta, tb, seq):
    T = x2.shape[0]
    tm = PROJ_TM
    n_pos_blocks = seq // tm

    def table_blk(i, j):
        kind = jnp.where(j == 3, 0, jnp.where(j == 4, 1, jnp.where(j == 0, 2, 3)))
        return (kind, i % n_pos_blocks, 0)

    return pl.pallas_call(
        _qkv_kernel,
        grid=(T // tm, 6),
        in_specs=[
            pl.BlockSpec((tm, D_MODEL), lambda i, j: (i, 0)),
            pl.BlockSpec((1, D_MODEL), lambda i, j: (0, 0)),
            pl.BlockSpec((D_MODEL, SECTION), lambda i, j: (0, j)),
            pl.BlockSpec((None, tm, HEAD_DIM), table_blk),
            pl.BlockSpec((None, tm, HEAD_DIM), table_blk),
        ],
        out_specs=pl.BlockSpec((tm, SECTION), lambda i, j: (i, j)),
        out_shape=jax.ShapeDtypeStruct((T, 3 * (D_SB + D_DIL)), BF16),
        scratch_shapes=[pltpu.VMEM((tm, D_MODEL), BF16)],
        compiler_params=pltpu.CompilerParams(
            dimension_semantics=("arbitrary", "arbitrary"), vmem_limit_bytes=VMEM_LIMIT),
        name="qkv_proj",
    )(x2, gain, w_in, ta, tb)


def _sb_kernel(q_ref, k_ref, v_ref, g_ref, uu_ref, o_ref, c_ref, acc_ref):
    seq = q_ref.shape[0]
    tq, tk = SB_TQ, SB_TK
    nt = (((1,), (1,)), ((), ()))
    heads = range(SB_HEADS)
    hs = [slice(h * HEAD_DIM, (h + 1) * HEAD_DIM) for h in heads]

    def aligned(row):
        return row if isinstance(row, int) else pl.multiple_of(row, tq)

    def key_tile(qb, k_start, width, ends_on_diagonal):
        n_sub = width // tk
        q0 = aligned(qb * tq)
        k0 = aligned(k_start)
        z = [lax.dot_general(q_ref[pl.ds(q0, tq), hs[h]], k_ref[pl.ds(k0, width), hs[h]], nt,
                             preferred_element_type=F32) for h in heads]
        log_beta = [jnp.minimum(z[h], 0.0) - jnp.log2(1.0 + jnp.exp2(-jnp.abs(z[h]))) for h in heads]
        log_keep = [log_beta[h] - z[h] for h in heads]
        masked = ends_on_diagonal
        if masked:
            causal = (lax.broadcasted_iota(jnp.int32, (tq, width), 0) + (width - tq)
                      > lax.broadcasted_iota(jnp.int32, (tq, width), 1))
            log_keep = [jnp.where(causal, lk, 0.0) for lk in log_keep]
        hi = [lk.astype(BF16) for lk in log_keep]
        lo = [(log_keep[h] - hi[h].astype(F32)).astype(BF16) for h in heads]
        r = [[jnp.dot(jnp.concatenate([hi[h][:, j * tk:(j + 1) * tk], lo[h][:, j * tk:(j + 1) * tk]], axis=1),
                      uu_ref[...], preferred_element_type=F32) for j in range(n_sub)] for h in heads]
        pv = []
        for h in heads:
            c = c_ref[h]
            a = [None] * n_sub
            for j in range(n_sub - 1, -1, -1):
                a[j] = jnp.exp2(log_beta[h][:, j * tk:(j + 1) * tk] + r[h][j][:, :tk] + c)
                c = c + r[h][j][:, tk:]
            c_ref[h] = c
            a = jnp.concatenate(a, axis=1)
            if masked:
                a = jnp.where(causal, a, 0.0)
            pv.append(jnp.dot(a.astype(BF16), v_ref[pl.ds(k0, width), hs[h]], preferred_element_type=F32))
        for h in heads:
            acc_ref[h] += pv[h]

    def c_max():
        return jnp.max(functools.reduce(jnp.maximum, [c_ref[h] for h in heads]))

    def finish(qb):
        q0 = aligned(qb * tq)
        for h in heads:
            o_ref[pl.ds(q0, tq), hs[h]] = _rms(acc_ref[h], g_ref[:, hs[h]]).astype(o_ref.dtype)

    def q_tile(qb, _):
        c_ref[...] = jnp.zeros_like(c_ref)
        acc_ref[...] = jnp.zeros_like(acc_ref)
        key_tile(qb, (qb - 1) * tq, 2 * tq, True)

        def live(carry):
            kt, c_hi = carry
            return (kt >= 0) & (c_hi > SB_DEAD_LOG2)

        def earlier(carry):
            kt, _ = carry
            key_tile(qb, kt * tq, tq, False)
            return kt - 1, c_max()

        lax.while_loop(live, earlier, (qb - 2, c_max()))
        finish(qb)
        return 0

    c_ref[...] = jnp.zeros_like(c_ref)
    acc_ref[...] = jnp.zeros_like(acc_ref)
    key_tile(0, 0, tq, True)
    finish(0)
    lax.fori_loop(1, seq // tq, q_tile, 0)


def _sb_attention(proj, gain, batch, seq):
    width = SB_HEADS * HEAD_DIM
    n_groups = D_SB // width
    p3 = proj.reshape(batch, seq, proj.shape[-1])
    tri = (jnp.arange(SB_TK)[:, None] > jnp.arange(SB_TK)[None, :])
    u = jnp.concatenate([tri.astype(BF16), jnp.ones((SB_TK, SB_TK), BF16)], axis=1)
    uu = jnp.concatenate([u, u], axis=0)
    out = pl.pallas_call(
        _sb_kernel,
        grid=(batch, n_groups),
        in_specs=[
            pl.BlockSpec((None, seq, width), lambda b, g: (b, 0, g)),
            pl.BlockSpec((None, seq, width), lambda b, g: (b, 0, n_groups + g)),
            pl.BlockSpec((None, seq, width), lambda b, g: (b, 0, 2 * n_groups + g)),
            pl.BlockSpec((1, width), lambda b, g: (0, g)),
            pl.BlockSpec((2 * SB_TK, 2 * SB_TK), lambda b, g: (0, 0)),
        ],
        out_specs=pl.BlockSpec((None, seq, width), lambda b, g: (b, 0, g)),
        out_shape=jax.ShapeDtypeStruct((batch, seq, D_SB), BF16),
        scratch_shapes=[pltpu.VMEM((SB_HEADS, SB_TQ, SB_TK), F32),
                        pltpu.VMEM((SB_HEADS, SB_TQ, HEAD_DIM), F32)],
        compiler_params=pltpu.CompilerParams(
            dimension_semantics=("arbitrary", "arbitrary"), vmem_limit_bytes=VMEM_LIMIT),
        name="sb_attn",
    )(p3, p3, p3, gain, uu)
    return out.reshape(batch * seq, D_SB)


def _dil_kernel(q_ref, k_ref, v_ref, g_ref, o_ref, qkv_s, out_s, lse_s):
    seq = o_ref.shape[0]
    for p, ref in enumerate((q_ref, k_ref, v_ref)):
        for h in range(DIL_HEADS):
            qkv_s[p * DIL_HEADS + h] = ref[:, h * HEAD_DIM:(h + 1) * HEAD_DIM].astype(F32)
    q_refs, k_refs, v_refs = ([qkv_s.at[p * DIL_HEADS + h] for h in range(DIL_HEADS)] for p in range(3))
    qb = DIL_QB
    nt = (((1,), (1,)), ((), ()))
    i2 = lax.broadcasted_iota(jnp.int32, (qb, 2 * qb), 0)
    j2 = lax.broadcasted_iota(jnp.int32, (qb, 2 * qb), 1)
    band = (j2 >= i2) & (j2 <= i2 + N_BACK)
    i1 = lax.broadcasted_iota(jnp.int32, (qb, qb), 0)
    j1 = lax.broadcasted_iota(jnp.int32, (qb, qb), 1)
    tril = j1 <= i1

    def rows(start, d):
        return pl.ds(start, qb, stride=d) if d > 1 else pl.ds(pl.multiple_of(start, qb), qb)

    def block_results(i, d):
        n_blocks = seq // d // qb
        chains = [(h, i * DIL_UNROLL + u) for h in range(DIL_HEADS) for u in range(DIL_UNROLL)]
        cur, q, k, v, valid = [], [], [], [], []
        for h, g in chains:
            r = g // n_blocks
            n = g % n_blocks
            c0 = r + n * (qb * d)
            cur.append(c0)
            q.append(q_refs[h][rows(c0, d), :].astype(BF16))
            if n_blocks > 1:
                p0 = r + jnp.maximum(n - 1, 0) * (qb * d)
                k.append(jnp.concatenate([k_refs[h][rows(p0, d), :], k_refs[h][rows(c0, d), :]], axis=0))
                v.append(jnp.concatenate([v_refs[h][rows(p0, d), :], v_refs[h][rows(c0, d), :]], axis=0))
                valid.append(band & (j2 >= jnp.where(n > 0, 0, qb)))
            else:
                k.append(k_refs[h][rows(c0, d), :])
                v.append(v_refs[h][rows(c0, d), :])
                valid.append(tril)
        idx = range(len(chains))
        s = [lax.dot_general(q[c], k[c].astype(BF16), nt, preferred_element_type=F32) for c in idx]
        s = [jnp.where(valid[c], s[c], -jnp.inf) for c in idx]
        m = [jnp.max(s[c], axis=1, keepdims=True) for c in idx]
        p = [jnp.exp(s[c] - m[c]) for c in idx]
        den = [jnp.sum(p[c], axis=1, keepdims=True) for c in idx]
        acc = [jnp.dot(p[c].astype(BF16), v[c].astype(BF16), preferred_element_type=F32) for c in idx]
        return chains, cur, [acc[c] * (1.0 / den[c]) for c in idx], [m[c] + jnp.log(den[c]) for c in idx]

    strided = [d for d in DILATIONS if d > 1]
    for slot, d in enumerate(strided):
        def sub_blocks(i, _, slot=slot, d=d):
            chains, cur, outs, lses = block_results(i, d)
            for c, (h, _) in enumerate(chains):
                out_s[h * len(strided) + slot, rows(cur[c], d), :] = outs[c]
                lse_s[h * len(strided) + slot, rows(cur[c], d), :] = jnp.broadcast_to(lses[c], (qb, HEAD_DIM))
            return 0

        lax.fori_loop(0, seq // qb // DIL_UNROLL, sub_blocks, 0)

    def merge(i, _):
        chains, cur, outs, lses = block_results(i, 1)
        for c, (h, _) in enumerate(chains):
            rs = rows(cur[c], 1)
            hs = slice(h * HEAD_DIM, (h + 1) * HEAD_DIM)
            lse = [jnp.broadcast_to(lses[c], (qb, HEAD_DIM))]
            lse += [lse_s[h * len(strided) + slot, rs, :] for slot in range(len(strided))]
            branch = [outs[c]] + [out_s[h * len(strided) + slot, rs, :] for slot in range(len(strided))]
            mx = functools.reduce(jnp.maximum, lse)
            w = [jnp.exp(l - mx) for l in lse]
            num = sum(w_b * o_b for w_b, o_b in zip(w, branch))
            o_ref[rs, hs] = _rms(num / sum(w), g_ref[:, hs]).astype(o_ref.dtype)
        return 0

    lax.fori_loop(0, seq // qb // DIL_UNROLL, merge, 0)


def _dil_attention(proj, gain, batch, seq):
    width = DIL_HEADS * HEAD_DIM
    n_groups = D_DIL // width
    p3 = proj.reshape(batch, seq, proj.shape[-1])
    out = pl.pallas_call(
        _dil_kernel,
        grid=(batch, n_groups),
        in_specs=[
            pl.BlockSpec((None, seq, width),
                         functools.partial(lambda b, g, *, off: (b, 0, off + g),
                                           off=(3 * D_SB + part * D_DIL) // width))
            for part in range(3)
        ] + [pl.BlockSpec((1, width), lambda b, g: (0, g))],
        out_specs=pl.BlockSpec((None, seq, width), lambda b, g: (b, 0, g)),
        out_shape=jax.ShapeDtypeStruct((batch, seq, D_DIL), BF16),
        scratch_shapes=[pltpu.VMEM((3 * DIL_HEADS, seq, HEAD_DIM), F32)]
        + [pltpu.VMEM((DIL_HEADS * (len(DILATIONS) - 1), seq, HEAD_DIM), F32)] * 2,
        compiler_params=pltpu.CompilerParams(
            dimension_semantics=("arbitrary", "arbitrary"), vmem_limit_bytes=VMEM_LIMIT),
        name="dil_attn",
    )(p3, p3, p3, gain)
    return out.reshape(batch * seq, D_DIL)


def _out_kernel(sb_ref, dl_ref, w_ref, x_ref, gpost_ref, gpre_ref, x1_ref, h2_ref):
    mix = jnp.dot(sb_ref[...], w_ref[:D_SB, :], preferred_element_type=F32)
    mix += jnp.dot(dl_ref[...], w_ref[D_SB:, :], preferred_element_type=F32)
    x1 = x_ref[...] + _rms(mix, gpost_ref[...])
    x1_ref[...] = x1
    h2_ref[...] = _rms(x1, gpre_ref[...]).astype(BF16)


def _out_proj(mixed_sb, mixed_dl, w_out, x2, g_post, g_pre_ffn):
    T = x2.shape[0]
    tm = OUT_TM
    return pl.pallas_call(
        _out_kernel,
        grid=(T // tm,),
        in_specs=[
            pl.BlockSpec((tm, D_SB), lambda i: (i, 0)),
            pl.BlockSpec((tm, D_DIL), lambda i: (i, 0)),
            pl.BlockSpec((D_SB + D_DIL, D_MODEL), lambda i: (0, 0), pipeline_mode=pl.Buffered(1)),
            pl.BlockSpec((tm, D_MODEL), lambda i: (i, 0)),
            pl.BlockSpec((1, D_MODEL), lambda i: (0, 0)),
            pl.BlockSpec((1, D_MODEL), lambda i: (0, 0)),
        ],
        out_specs=[
            pl.BlockSpec((tm, D_MODEL), lambda i: (i, 0)),
            pl.BlockSpec((tm, D_MODEL), lambda i: (i, 0)),
        ],
        out_shape=[
            jax.ShapeDtypeStruct((T, D_MODEL), F32),
            jax.ShapeDtypeStruct((T, D_MODEL), BF16),
        ],
        compiler_params=pltpu.CompilerParams(
            dimension_semantics=("arbitrary",), vmem_limit_bytes=VMEM_LIMIT),
        name="out_proj",
    )(mixed_sb, mixed_dl, w_out, x2, g_post, g_pre_ffn)


def _ffn_kernel(h_ref, wg_ref, wv_ref, cwg_ref, cwv_ref, cbg_ref, cbv_ref, wdn_ref, x1_hbm, g_ref,
                o_ref, tail_ref, x1_ref, x1_sem, *, tiles_per_seq):
    i = pl.program_id(0)
    f = pl.program_id(1)
    tm, halo = FFN_TM, FFN_HALO

    @pl.when((i == 0) & (f == 0))
    def _():
        tail_ref[...] = jnp.zeros_like(tail_ref)

    def x1_copy():
        rows = pl.ds(pl.multiple_of(i * tm, tm), tm)
        return pltpu.make_async_copy(x1_hbm.at[rows, :], x1_ref, x1_sem)

    @pl.when(f == 0)
    def _():
        x1_copy().start()
        o_ref[...] = jnp.zeros_like(o_ref)

    keep = (i % tiles_per_seq) != 0

    def up_conv(part, w_ref, cw_ref, cb_ref):
        u = jnp.dot(h_ref[...], w_ref[...], preferred_element_type=F32)
        prev = jnp.where(keep, tail_ref[f, part], 0.0)
        tail_ref[f, part] = u[tm - halo:, :]
        u = jnp.concatenate([prev, u], axis=0)
        cw = cw_ref[...]
        conv = cb_ref[...] + cw[CONV_WIDTH - 1:CONV_WIDTH, :] * u[halo:, :]
        for tap in range(CONV_WIDTH - 1):
            back = CONV_WIDTH - 1 - tap
            conv += cw[tap:tap + 1, :] * u[halo - back:halo - back + tm, :]
        return conv

    gate = up_conv(0, wg_ref, cwg_ref, cbg_ref)
    value = up_conv(1, wv_ref, cwv_ref, cbv_ref)
    y = jax.nn.gelu(gate, approximate=True) * value
    o_ref[...] += jnp.dot(y.astype(BF16), wdn_ref[...], preferred_element_type=F32)

    @pl.when(f == pl.num_programs(1) - 1)
    def _():
        x1_copy().wait()
        o_ref[...] = x1_ref[...] + _rms(o_ref[...], g_ref[...])


def _conv_ffn(h2, w_up_p, conv_w_p, conv_b_p, w_down_p, x1, g_post, seq):
    T = h2.shape[0]
    tm, fc, halo = FFN_TM, FFN_FC, FFN_HALO
    n_chunks = D_FF_PAD // fc
    gate_blk = lambda i, f: (0, f)
    value_blk = lambda i, f: (0, n_chunks + f)
    return pl.pallas_call(
        functools.partial(_ffn_kernel, tiles_per_seq=seq // tm),
        grid=(T // tm, n_chunks),
        in_specs=[
            pl.BlockSpec((tm, D_MODEL), lambda i, f: (i, 0)),
            pl.BlockSpec((D_MODEL, fc), gate_blk),
            pl.BlockSpec((D_MODEL, fc), value_blk),
            pl.BlockSpec((CONV_WIDTH, fc), gate_blk),
            pl.BlockSpec((CONV_WIDTH, fc), value_blk),
            pl.BlockSpec((1, fc), gate_blk),
            pl.BlockSpec((1, fc), value_blk),
            pl.BlockSpec((fc, D_MODEL), lambda i, f: (f, 0)),
            pl.BlockSpec(memory_space=pl.ANY),
            pl.BlockSpec((1, D_MODEL), lambda i, f: (0, 0)),
        ],
        out_specs=pl.BlockSpec((tm, D_MODEL), lambda i, f: (i, 0)),
        out_shape=jax.ShapeDtypeStruct((T, D_MODEL), F32),
        scratch_shapes=[pltpu.VMEM((n_chunks, 2, halo, fc), F32),
                        pltpu.VMEM((tm, D_MODEL), F32),
                        pltpu.SemaphoreType.DMA(())],
        compiler_params=pltpu.CompilerParams(
            dimension_semantics=("arbitrary", "arbitrary"), vmem_limit_bytes=FFN_VMEM_LIMIT),
        name="conv_ffn",
    )(h2, w_up_p, w_up_p, conv_w_p, conv_w_p, conv_b_p, conv_b_p, w_down_p, x1, g_post)


def _pad_gate_value(a, dtype):
    zeros = jnp.zeros((a.shape[0], D_FF_PAD - D_FF), dtype)
    return jnp.concatenate([a[:, :D_FF].astype(dtype), zeros, a[:, D_FF:].astype(dtype), zeros], axis=1)


def _stage_up_kernel(w_ref, o_ref):
    rows = w_ref.shape[0]
    for part in range(2):
        o_ref[:, part * D_FF_PAD:part * D_FF_PAD + D_FF] = w_ref[:, part * D_FF:(part + 1) * D_FF].astype(BF16)
        o_ref[:, part * D_FF_PAD + D_FF:(part + 1) * D_FF_PAD] = jnp.zeros((rows, D_FF_PAD - D_FF), BF16)


def _stage_down_kernel(w_ref, o_ref):
    o_ref[:D_FF, :] = w_ref[...].astype(BF16)
    o_ref[D_FF:, :] = jnp.zeros((D_FF_PAD - D_FF, w_ref.shape[1]), BF16)


def _stage_ffn_weights(w_up, w_down):
    rows, cols = STAGE_UP_ROWS, STAGE_DOWN_COLS
    params = pltpu.CompilerParams(dimension_semantics=("arbitrary",), vmem_limit_bytes=VMEM_LIMIT)
    w_up_p = pl.pallas_call(
        _stage_up_kernel,
        grid=(D_MODEL // rows,),
        in_specs=[pl.BlockSpec((rows, 2 * D_FF), lambda i: (i, 0))],
        out_specs=pl.BlockSpec((rows, 2 * D_FF_PAD), lambda i: (i, 0)),
        out_shape=jax.ShapeDtypeStruct((D_MODEL, 2 * D_FF_PAD), BF16),
        compiler_params=params,
        name="stage_w_up",
    )(w_up)
    w_down_p = pl.pallas_call(
        _stage_down_kernel,
        grid=(D_MODEL // cols,),
        in_specs=[pl.BlockSpec((D_FF, cols), lambda i: (0, i))],
        out_specs=pl.BlockSpec((D_FF_PAD, cols), lambda i: (0, i)),
        out_shape=jax.ShapeDtypeStruct((D_FF_PAD, D_MODEL), BF16),
        compiler_params=params,
        name="stage_w_down",
    )(w_down)
    return w_up_p, w_down_p


def kernel(x, pre_mix_gain, post_mix_gain, pre_ffn_gain, post_ffn_gain, w_in, sb_out_gain, dil_out_gain,
           w_out, w_up, conv_w, conv_b, w_down):
    batch, seq, _ = x.shape
    depth = w_in.shape[0]
    ta, tb = _section_tables(seq)
    x2 = x.reshape(batch * seq, D_MODEL)
    for layer in range(depth):
        row = lambda g: g[layer][None, :]
        proj = _qkv_proj(x2, row(pre_mix_gain), w_in[layer].astype(BF16), ta, tb, seq)
        mixed_sb = _sb_attention(proj, row(sb_out_gain), batch, seq)
        mixed_dl = _dil_attention(proj, row(dil_out_gain), batch, seq)
        x1, h2 = _out_proj(mixed_sb, mixed_dl, w_out[layer].astype(BF16), x2,
                           row(post_mix_gain), row(pre_ffn_gain))
        w_up_p, w_down_p = _stage_ffn_weights(w_up[layer], w_down[layer])
        x2 = _conv_ffn(h2, w_up_p, _pad_gate_value(conv_w[layer], F32), _pad_gate_value(row(conv_b), F32),
                       w_down_p, x1, row(post_ffn_gain), seq)
    return x2.reshape(batch, seq, D_MODEL)
```

```python
import functools
import math

import jax
import jax.numpy as jnp
from jax import lax
from jax.experimental import pallas as pl
from jax.experimental.pallas import tpu as pltpu

F32 = jnp.float32
BF16 = jnp.bfloat16

D_MODEL = 2048
HEAD_DIM = 128
N_HEADS_SB = 8
N_HEADS_DIL = 8
D_SB = N_HEADS_SB * HEAD_DIM
D_DIL = N_HEADS_DIL * HEAD_DIM
DILATIONS = (1, 4, 16)
N_BACK = 128
D_FF = 5504
CONV_WIDTH = 3
ROPE_THETA = 10000.0
RMS_EPS = 1e-6
SCALE = HEAD_DIM ** -0.5
LOG2_E = math.log2(math.e)

LANES = 128
VMEM_LIMIT = 56 * 1024 * 1024
FFN_VMEM_LIMIT = 60 * 1024 * 1024

PROJ_TM = 1024
OUT_TM = 512
OUT_ROW_GROUPS = 4
SECTION = 1024
SB_HEADS = 4
SB_TQ = 256
SB_TK = 128
SB_DEAD_LOG2 = -152.0
DIL_HEADS = 2
DIL_QB = 128
DIL_UNROLL = 4
FFN_TM = 1024
FFN_FC = 512
FFN_HALO = 8
D_FF_PAD = -(-D_FF // FFN_FC) * FFN_FC
STAGE_UP_ROWS = 128
STAGE_DOWN_COLS = 256


def _rms(x, gain):
    return x * lax.rsqrt(jnp.mean(x * x, axis=-1, keepdims=True) + RMS_EPS) * gain


def _qkv_kernel(x_ref, g_ref, w_ref, ta_ref, tb_ref, o_ref, h_ref):
    j = pl.program_id(1)

    @pl.when(j == 0)
    def _():
        h_ref[...] = _rms(x_ref[...], g_ref[...]).astype(BF16)

    acc = jnp.dot(h_ref[...], w_ref[...], preferred_element_type=F32)
    ta, tb = ta_ref[...], tb_ref[...]
    for h in range(SECTION // HEAD_DIM):
        hs = slice(h * HEAD_DIM, (h + 1) * HEAD_DIM)
        a_h = acc[:, hs]
        o_ref[:, hs] = (a_h * ta + pltpu.roll(a_h, HEAD_DIM // 2, axis=1) * tb).astype(BF16)


def _section_tables(seq):
    inv_freq = ROPE_THETA ** (-jnp.arange(0, HEAD_DIM, 2, dtype=F32) / HEAD_DIM)
    ang = jnp.arange(seq, dtype=F32)[:, None] * inv_freq[None, :]
    cos, sin = jnp.cos(ang), jnp.sin(ang)
    cos_t = jnp.concatenate([cos, cos], axis=-1)
    sin_t = jnp.concatenate([-sin, sin], axis=-1)
    one, zero = jnp.ones_like(cos_t), jnp.zeros_like(cos_t)
    ta = jnp.stack([cos_t * SCALE, cos_t, one * (SCALE * LOG2_E), one])
    tb = jnp.stack([sin_t * SCALE, sin_t, zero, zero])
    return ta, tb


def _qkv_proj(x2, gain, w_in, ta, tb, seq):
    T = x2.shape[0]
    tm = PROJ_TM
    n_pos_blocks = seq // tm

    def table_blk(i, j):
        kind = jnp.where(j == 3, 0, jnp.where(j == 4, 1, jnp.where(j == 0, 2, 3)))
        return (kind, i % n_pos_blocks, 0)

    return pl.pallas_call(
        _qkv_kernel,
        grid=(T // tm, 6),
        in_specs=[
            pl.BlockSpec((tm, D_MODEL), lambda i, j: (i, 0)),
            pl.BlockSpec((1, D_MODEL), lambda i, j: (0, 0)),
            pl.BlockSpec((D_MODEL, SECTION), lambda i, j: (0, j)),
            pl.BlockSpec((None, tm, HEAD_DIM), table_blk),
            pl.BlockSpec((None, tm, HEAD_DIM), table_blk),
        ],
        out_specs=pl.BlockSpec((tm, SECTION), lambda i, j: (i, j)),
        out_shape=jax.ShapeDtypeStruct((T, 3 * (D_SB + D_DIL)), BF16),
        scratch_shapes=[pltpu.VMEM((tm, D_MODEL), BF16)],
        compiler_params=pltpu.CompilerParams(
            dimension_semantics=("arbitrary", "arbitrary"), vmem_limit_bytes=VMEM_LIMIT),
        name="qkv_proj",
    )(x2, gain, w_in, ta, tb)


def _sb_kernel(q_ref, k_ref, v_ref, g_ref, uu_ref, o_ref, c_ref, acc_ref):
    seq = q_ref.shape[0]
    tq, tk = SB_TQ, SB_TK
    nt = (((1,), (1,)), ((), ()))
    heads = range(SB_HEADS)
    hs = [slice(h * HEAD_DIM, (h + 1) * HEAD_DIM) for h in heads]

    def aligned(row):
        return row if isinstance(row, int) else pl.multiple_of(row, tq)

    def key_tile(qb, k_start, width, ends_on_diagonal):
        n_sub = width // tk
        q0 = aligned(qb * tq)
        k0 = aligned(k_start)
        z = [lax.dot_general(q_ref[pl.ds(q0, tq), hs[h]], k_ref[pl.ds(k0, width), hs[h]], nt,
                             preferred_element_type=F32) for h in heads]
        log_beta = [jnp.minimum(z[h], 0.0) - jnp.log2(1.0 + jnp.exp2(-jnp.abs(z[h]))) for h in heads]
        log_keep = [log_beta[h] - z[h] for h in heads]
        masked = ends_on_diagonal
        if masked:
            causal = (lax.broadcasted_iota(jnp.int32, (tq, width), 0) + (width - tq)
                      > lax.broadcasted_iota(jnp.int32, (tq, width), 1))
            log_keep = [jnp.where(causal, lk, 0.0) for lk in log_keep]
        hi = [lk.astype(BF16) for lk in log_keep]
        lo = [(log_keep[h] - hi[h].astype(F32)).astype(BF16) for h in heads]
        r = [[jnp.dot(jnp.concatenate([hi[h][:, j * tk:(j + 1) * tk], lo[h][:, j * tk:(j + 1) * tk]], axis=1),
                      uu_ref[...], preferred_element_type=F32) for j in range(n_sub)] for h in heads]
        pv = []
        for h in heads:
            c = c_ref[h]
            a = [None] * n_sub
            for j in range(n_sub - 1, -1, -1):
                a[j] = jnp.exp2(log_beta[h][:, j * tk:(j + 1) * tk] + r[h][j][:, :tk] + c)
                c = c + r[h][j][:, tk:]
            c_ref[h] = c
            a = jnp.concatenate(a, axis=1)
            if masked:
                a = jnp.where(causal, a, 0.0)
            pv.append(jnp.dot(a.astype(BF16), v_ref[pl.ds(k0, width), hs[h]], preferred_element_type=F32))
        for h in heads:
            acc_ref[h] += pv[h]

    def c_max():
        return jnp.max(functools.reduce(jnp.maximum, [c_ref[h] for h in heads]))

    def finish(qb):
        q0 = aligned(qb * tq)
        for h in heads:
            o_ref[pl.ds(q0, tq), hs[h]] = _rms(acc_ref[h], g_ref[:, hs[h]]).astype(o_ref.dtype)

    def q_tile(qb, _):
        c_ref[...] = jnp.zeros_like(c_ref)
        acc_ref[...] = jnp.zeros_like(acc_ref)
        key_tile(qb, (qb - 1) * tq, 2 * tq, True)

        def live(carry):
            kt, c_hi = carry
            return (kt >= 0) & (c_hi > SB_DEAD_LOG2)

        def earlier(carry):
            kt, _ = carry
            key_tile(qb, kt * tq, tq, False)
            return kt - 1, c_max()

        lax.while_loop(live, earlier, (qb - 2, c_max()))
        finish(qb)
        return 0

    c_ref[...] = jnp.zeros_like(c_ref)
    acc_ref[...] = jnp.zeros_like(acc_ref)
    key_tile(0, 0, tq, True)
    finish(0)
    lax.fori_loop(1, seq // tq, q_tile, 0)


def _sb_attention(proj, gain, batch, seq):
    width = SB_HEADS * HEAD_DIM
    n_groups = D_SB // width
    p3 = proj.reshape(batch, seq, proj.shape[-1])
    tri = (jnp.arange(SB_TK)[:, None] > jnp.arange(SB_TK)[None, :])
    u = jnp.concatenate([tri.astype(BF16), jnp.ones((SB_TK, SB_TK), BF16)], axis=1)
    uu = jnp.concatenate([u, u], axis=0)
    out = pl.pallas_call(
        _sb_kernel,
        grid=(batch, n_groups),
        in_specs=[
            pl.BlockSpec((None, seq, width), lambda b, g: (b, 0, g)),
            pl.BlockSpec((None, seq, width), lambda b, g: (b, 0, n_groups + g)),
            pl.BlockSpec((None, seq, width), lambda b, g: (b, 0, 2 * n_groups + g)),
            pl.BlockSpec((1, width), lambda b, g: (0, g)),
            pl.BlockSpec((2 * SB_TK, 2 * SB_TK), lambda b, g: (0, 0)),
        ],
        out_specs=pl.BlockSpec((None, seq, width), lambda b, g: (b, 0, g)),
        out_shape=jax.ShapeDtypeStruct((batch, seq, D_SB), BF16),
        scratch_shapes=[pltpu.VMEM((SB_HEADS, SB_TQ, SB_TK), F32),
                        pltpu.VMEM((SB_HEADS, SB_TQ, HEAD_DIM), F32)],
        compiler_params=pltpu.CompilerParams(
            dimension_semantics=("arbitrary", "arbitrary"), vmem_limit_bytes=VMEM_LIMIT),
        name="sb_attn",
    )(p3, p3, p3, gain, uu)
    return out.reshape(batch * seq, D_SB)


def _dil_kernel(*refs):
    g_ref, o_ref, qkv_s, out_s, lse_s = refs[3 * DIL_HEADS:]
    seq = o_ref.shape[0]
    for i in range(3 * DIL_HEADS):
        qkv_s[i] = refs[i][...].astype(F32)
    q_refs, k_refs, v_refs = ([qkv_s.at[p * DIL_HEADS + h] for h in range(DIL_HEADS)] for p in range(3))
    qb = DIL_QB
    nt = (((1,), (1,)), ((), ()))
    i2 = lax.broadcasted_iota(jnp.int32, (qb, 2 * qb), 0)
    j2 = lax.broadcasted_iota(jnp.int32, (qb, 2 * qb), 1)
    band = (j2 >= i2) & (j2 <= i2 + N_BACK)
    i1 = lax.broadcasted_iota(jnp.int32, (qb, qb), 0)
    j1 = lax.broadcasted_iota(jnp.int32, (qb, qb), 1)
    tril = j1 <= i1

    def rows(start, d):
        return pl.ds(start, qb, stride=d) if d > 1 else pl.ds(pl.multiple_of(start, qb), qb)

    def block_results(i, d):
        n_blocks = seq // d // qb
        chains = [(h, i * DIL_UNROLL + u) for h in range(DIL_HEADS) for u in range(DIL_UNROLL)]
        cur, q, k, v, valid = [], [], [], [], []
        for h, g in chains:
            r = g // n_blocks
            n = g % n_blocks
            c0 = r + n * (qb * d)
            cur.append(c0)
            q.append(q_refs[h][rows(c0, d), :].astype(BF16))
            if n_blocks > 1:
                p0 = r + jnp.maximum(n - 1, 0) * (qb * d)
                k.append(jnp.concatenate([k_refs[h][rows(p0, d), :], k_refs[h][rows(c0, d), :]], axis=0))
                v.append(jnp.concatenate([v_refs[h][rows(p0, d), :], v_refs[h][rows(c0, d), :]], axis=0))
                valid.append(band & (j2 >= jnp.where(n > 0, 0, qb)))
            else:
                k.append(k_refs[h][rows(c0, d), :])
                v.append(v_refs[h][rows(c0, d), :])
                valid.append(tril)
        idx = range(len(chains))
        s = [lax.dot_general(q[c], k[c].astype(BF16), nt, preferred_element_type=F32) for c in idx]
        s = [jnp.where(valid[c], s[c], -jnp.inf) for c in idx]
        m = [jnp.max(s[c], axis=1, keepdims=True) for c in idx]
        p = [jnp.exp(s[c] - m[c]) for c in idx]
        den = [jnp.sum(p[c], axis=1, keepdims=True) for c in idx]
        acc = [jnp.dot(p[c].astype(BF16), v[c].astype(BF16), preferred_element_type=F32) for c in idx]
        return chains, cur, [acc[c] * (1.0 / den[c]) for c in idx], [m[c] + jnp.log(den[c]) for c in idx]

    strided = [d for d in DILATIONS if d > 1]
    for slot, d in enumerate(strided):
        def sub_blocks(i, _, slot=slot, d=d):
            chains, cur, outs, lses = block_results(i, d)
            for c, (h, _) in enumerate(chains):
                out_s[h * len(strided) + slot, rows(cur[c], d), :] = outs[c]
                lse_s[h * len(strided) + slot, rows(cur[c], d), :] = jnp.broadcast_to(lses[c], (qb, HEAD_DIM))
            return 0

        lax.fori_loop(0, seq // qb // DIL_UNROLL, sub_blocks, 0)

    def merge(i, _):
        chains, cur, outs, lses = block_results(i, 1)
        for c, (h, _) in enumerate(chains):
            rs = rows(cur[c], 1)
            hs = slice(h * HEAD_DIM, (h + 1) * HEAD_DIM)
            lse = [jnp.broadcast_to(lses[c], (qb, HEAD_DIM))]
            lse += [lse_s[h * len(strided) + slot, rs, :] for slot in range(len(strided))]
            branch = [outs[c]] + [out_s[h * len(strided) + slot, rs, :] for slot in range(len(strided))]
            mx = functools.reduce(jnp.maximum, lse)
            w = [jnp.exp(l - mx) for l in lse]
            num = sum(w_b * o_b for w_b, o_b in zip(w, branch))
            o_ref[rs, hs] = _rms(num / sum(w), g_ref[:, hs]).astype(o_ref.dtype)
        return 0

    lax.fori_loop(0, seq // qb // DIL_UNROLL, merge, 0)


def _dil_attention(proj, gain, batch, seq):
    width = DIL_HEADS * HEAD_DIM
    n_groups = D_DIL // width
    p3 = proj.reshape(batch, seq, proj.shape[-1])
    out = pl.pallas_call(
        _dil_kernel,
        grid=(batch, n_groups),
        in_specs=[
            pl.BlockSpec((None, seq, HEAD_DIM),
                         functools.partial(lambda b, g, *, off: (b, 0, off + g * DIL_HEADS),
                                           off=(3 * N_HEADS_SB + part * N_HEADS_DIL) + h))
            for part in range(3) for h in range(DIL_HEADS)
        ] + [pl.BlockSpec((1, width), lambda b, g: (0, g))],
        out_specs=pl.BlockSpec((None, seq, width), lambda b, g: (b, 0, g)),
        out_shape=jax.ShapeDtypeStruct((batch, seq, D_DIL), BF16),
        scratch_shapes=[pltpu.VMEM((3 * DIL_HEADS, seq, HEAD_DIM), F32)]
        + [pltpu.VMEM((DIL_HEADS * (len(DILATIONS) - 1), seq, HEAD_DIM), F32)] * 2,
        compiler_params=pltpu.CompilerParams(
            dimension_semantics=("arbitrary", "arbitrary"), vmem_limit_bytes=VMEM_LIMIT),
        name="dil_attn",
    )(*([p3] * (3 * DIL_HEADS)), gain)
    return out.reshape(batch * seq, D_DIL)


def _out_kernel(sb_ref, dl_ref, w_ref, x_ref, gpost_ref, gpre_ref, x1_ref, h2_ref):
    rows_per = sb_ref.shape[0] // OUT_ROW_GROUPS
    for r in range(OUT_ROW_GROUPS):
        rs = slice(r * rows_per, (r + 1) * rows_per)
        mix = jnp.dot(sb_ref[rs, :], w_ref[:D_SB, :], preferred_element_type=F32)
        mix += jnp.dot(dl_ref[rs, :], w_ref[D_SB:, :], preferred_element_type=F32)
        x1 = x_ref[rs, :] + _rms(mix, gpost_ref[...])
        x1_ref[rs, :] = x1
        h2_ref[rs, :] = _rms(x1, gpre_ref[...]).astype(BF16)


def _out_proj(mixed_sb, mixed_dl, w_out, x2, g_post, g_pre_ffn):
    T = x2.shape[0]
    tm = OUT_TM
    return pl.pallas_call(
        _out_kernel,
        grid=(T // tm,),
        in_specs=[
            pl.BlockSpec((tm, D_SB), lambda i: (i, 0)),
            pl.BlockSpec((tm, D_DIL), lambda i: (i, 0)),
            pl.BlockSpec((D_SB + D_DIL, D_MODEL), lambda i: (0, 0), pipeline_mode=pl.Buffered(1)),
            pl.BlockSpec((tm, D_MODEL), lambda i: (i, 0)),
            pl.BlockSpec((1, D_MODEL), lambda i: (0, 0)),
            pl.BlockSpec((1, D_MODEL), lambda i: (0, 0)),
        ],
        out_specs=[
            pl.BlockSpec((tm, D_MODEL), lambda i: (i, 0)),
            pl.BlockSpec((tm, D_MODEL), lambda i: (i, 0)),
        ],
        out_shape=[
            jax.ShapeDtypeStruct((T, D_MODEL), F32),
            jax.ShapeDtypeStruct((T, D_MODEL), BF16),
        ],
        compiler_params=pltpu.CompilerParams(
            dimension_semantics=("arbitrary",), vmem_limit_bytes=VMEM_LIMIT),
        name="out_proj",
    )(mixed_sb, mixed_dl, w_out, x2, g_post, g_pre_ffn)


def _ffn_kernel(h_ref, wg_ref, wv_ref, cwg_ref, cwv_ref, cbg_ref, cbv_ref, wdn_ref, x1_hbm, g_ref,
                o_ref, tail_ref, x1_ref, x1_sem, *, tiles_per_seq):
    i = pl.program_id(0)
    f = pl.program_id(1)
    tm, halo = FFN_TM, FFN_HALO

    @pl.when((i == 0) & (f == 0))
    def _():
        tail_ref[...] = jnp.zeros_like(tail_ref)

    def x1_copy():
        rows = pl.ds(pl.multiple_of(i * tm, tm), tm)
        return pltpu.make_async_copy(x1_hbm.at[rows, :], x1_ref, x1_sem)

    @pl.when(f == 0)
    def _():
        x1_copy().start()
        o_ref[...] = jnp.zeros_like(o_ref)

    keep = (i % tiles_per_seq) != 0

    def up_conv(part, w_ref, cw_ref, cb_ref):
        u = jnp.dot(h_ref[...], w_ref[...], preferred_element_type=F32)
        prev = jnp.where(keep, tail_ref[f, part], 0.0)
        tail_ref[f, part] = u[tm - halo:, :]
        u = jnp.concatenate([prev, u], axis=0)
        cw = cw_ref[...]
        conv = cb_ref[...] + cw[CONV_WIDTH - 1:CONV_WIDTH, :] * u[halo:, :]
        for tap in range(CONV_WIDTH - 1):
            back = CONV_WIDTH - 1 - tap
            conv += cw[tap:tap + 1, :] * u[halo - back:halo - back + tm, :]
        return conv

    gate = up_conv(0, wg_ref, cwg_ref, cbg_ref)
    value = up_conv(1, wv_ref, cwv_ref, cbv_ref)
    y = jax.nn.gelu(gate, approximate=True) * value
    o_ref[...] += jnp.dot(y.astype(BF16), wdn_ref[...], preferred_element_type=F32)

    @pl.when(f == pl.num_programs(1) - 1)
    def _():
        x1_copy().wait()
        o_ref[...] = x1_ref[...] + _rms(o_ref[...], g_ref[...])


def _conv_ffn(h2, w_up_p, conv_w_p, conv_b_p, w_down_p, x1, g_post, seq):
    T = h2.shape[0]
    tm, fc, halo = FFN_TM, FFN_FC, FFN_HALO
    n_chunks = D_FF_PAD // fc
    gate_blk = lambda i, f: (0, f)
    value_blk = lambda i, f: (0, n_chunks + f)
    return pl.pallas_call(
        functools.partial(_ffn_kernel, tiles_per_seq=seq // tm),
        grid=(T // tm, n_chunks),
        in_specs=[
            pl.BlockSpec((tm, D_MODEL), lambda i, f: (i, 0)),
            pl.BlockSpec((D_MODEL, fc), gate_blk),
            pl.BlockSpec((D_MODEL, fc), value_blk),
            pl.BlockSpec((CONV_WIDTH, fc), gate_blk),
            pl.BlockSpec((CONV_WIDTH, fc), value_blk),
            pl.BlockSpec((1, fc), gate_blk),
            pl.BlockSpec((1, fc), value_blk),
            pl.BlockSpec((fc, D_MODEL), lambda i, f: (f, 0)),
            pl.BlockSpec(memory_space=pl.ANY),
            pl.BlockSpec((1, D_MODEL), lambda i, f: (0, 0)),
        ],
        out_specs=pl.BlockSpec((tm, D_MODEL), lambda i, f: (i, 0)),
        out_shape=jax.ShapeDtypeStruct((T, D_MODEL), F32),
        scratch_shapes=[pltpu.VMEM((n_chunks, 2, halo, fc), F32),
                        pltpu.VMEM((tm, D_MODEL), F32),
                        pltpu.SemaphoreType.DMA(())],
        compiler_params=pltpu.CompilerParams(
            dimension_semantics=("arbitrary", "arbitrary"), vmem_limit_bytes=FFN_VMEM_LIMIT),
        name="conv_ffn",
    )(h2, w_up_p, w_up_p, conv_w_p, conv_w_p, conv_b_p, conv_b_p, w_down_p, x1, g_post)


def _pad_gate_value(a, dtype):
    zeros = jnp.zeros((a.shape[0], D_FF_PAD - D_FF), dtype)
    return jnp.concatenate([a[:, :D_FF].astype(dtype), zeros, a[:, D_FF:].astype(dtype), zeros], axis=1)


def _stage_up_kernel(w_ref, o_ref):
    rows = w_ref.shape[0]
    for part in range(2):
        o_ref[:, part * D_FF_PAD:part * D_FF_PAD + D_FF] = w_ref[:, part * D_FF:(part + 1) * D_FF].astype(BF16)
        o_ref[:, part * D_FF_PAD + D_FF:(part + 1) * D_FF_PAD] = jnp.zeros((rows, D_FF_PAD - D_FF), BF16)


def _stage_down_kernel(w_ref, o_ref):
    o_ref[:D_FF, :] = w_ref[...].astype(BF16)
    o_ref[D_FF:, :] = jnp.zeros((D_FF_PAD - D_FF, w_ref.shape[1]), BF16)


def _stage_ffn_weights(w_up, w_down):
    rows, cols = STAGE_UP_ROWS, STAGE_DOWN_COLS
    params = pltpu.CompilerParams(dimension_semantics=("arbitrary",), vmem_limit_bytes=VMEM_LIMIT)
    w_up_p = pl.pallas_call(
        _stage_up_kernel,
        grid=(D_MODEL // rows,),
        in_specs=[pl.BlockSpec((rows, 2 * D_FF), lambda i: (i, 0))],
        out_specs=pl.BlockSpec((rows, 2 * D_FF_PAD), lambda i: (i, 0)),
        out_shape=jax.ShapeDtypeStruct((D_MODEL, 2 * D_FF_PAD), BF16),
        compiler_params=params,
        name="stage_w_up",
    )(w_up)
    w_down_p = pl.pallas_call(
        _stage_down_kernel,
        grid=(D_MODEL // cols,),
        in_specs=[pl.BlockSpec((D_FF, cols), lambda i: (0, i))],
        out_specs=pl.BlockSpec((D_FF_PAD, cols), lambda i: (0, i)),
        out_shape=jax.ShapeDtypeStruct((D_FF_PAD, D_MODEL), BF16),
        compiler_params=params,
        name="stage_w_down",
    )(w_down)
    return w_up_p, w_down_p


def kernel(x, pre_mix_gain, post_mix_gain, pre_ffn_gain, post_ffn_gain, w_in, sb_out_gain, dil_out_gain,
           w_out, w_up, conv_w, conv_b, w_down):
    batch, seq, _ = x.shape
    depth = w_in.shape[0]
    ta, tb = _section_tables(seq)
    x2 = x.reshape(batch * seq, D_MODEL)
    for layer in range(depth):
        row = lambda g: g[layer][None, :]
        proj = _qkv_proj(x2, row(pre_mix_gain), w_in[layer].astype(BF16), ta, tb, seq)
        mixed_sb = _sb_attention(proj, row(sb_out_gain), batch, seq)
        mixed_dl = _dil_attention(proj, row(dil_out_gain), batch, seq)
        x1, h2 = _out_proj(mixed_sb, mixed_dl, w_out[layer].astype(BF16), x2,
                           row(post_mix_gain), row(pre_ffn_gain))
        w_up_p, w_down_p = _stage_ffn_weights(w_up[layer], w_down[layer])
        x2 = _conv_ffn(h2, w_up_p, _pad_gate_value(conv_w[layer], F32), _pad_gate_value(row(conv_b), F32),
                       w_down_p, x1, row(post_ffn_gain), seq)
    return x2.reshape(batch, seq, D_MODEL)
```

```python
import functools
import math

import jax
import jax.numpy as jnp
from jax import lax
from jax.experimental import pallas as pl
from jax.experimental.pallas import tpu as pltpu

F32 = jnp.float32
BF16 = jnp.bfloat16

D_MODEL = 2048
HEAD_DIM = 128
N_HEADS_SB = 8
N_HEADS_DIL = 8
D_SB = N_HEADS_SB * HEAD_DIM
D_DIL = N_HEADS_DIL * HEAD_DIM
DILATIONS = (1, 4, 16)
N_BACK = 128
D_FF = 5504
CONV_WIDTH = 3
ROPE_THETA = 10000.0
RMS_EPS = 1e-6
SCALE = HEAD_DIM ** -0.5
LOG2_E = math.log2(math.e)

LANES = 128
VMEM_LIMIT = 56 * 1024 * 1024
FFN_VMEM_LIMIT = 60 * 1024 * 1024

PROJ_TM = 1024
PROJ_ROW_GROUPS = 4
OUT_TM = 512
OUT_ROW_GROUPS = 4
SECTION = 1024
SB_HEADS = 4
SB_TQ = 256
SB_TK = 128
SB_DEAD_LOG2 = -152.0
DIL_HEADS = 2
DIL_QB = 128
DIL_UNROLL = 4
FFN_TM = 1024
FFN_FC = 512
FFN_HALO = 8
D_FF_PAD = -(-D_FF // FFN_FC) * FFN_FC
STAGE_UP_ROWS = 128
STAGE_DOWN_COLS = 256


def _rms(x, gain):
    return x * lax.rsqrt(jnp.mean(x * x, axis=-1, keepdims=True) + RMS_EPS) * gain


def _qkv_kernel(x_ref, g_ref, w_ref, ta_ref, tb_ref, o_ref, h_ref):
    j = pl.program_id(1)

    @pl.when(j == 0)
    def _():
        h_ref[...] = _rms(x_ref[...], g_ref[...]).astype(BF16)

    rows_per = h_ref.shape[0] // PROJ_ROW_GROUPS
    for r in range(PROJ_ROW_GROUPS):
        rs = slice(r * rows_per, (r + 1) * rows_per)
        acc = jnp.dot(h_ref[rs, :], w_ref[...], preferred_element_type=F32)
        ta, tb = ta_ref[rs, :], tb_ref[rs, :]
        for h in range(SECTION // HEAD_DIM):
            hs = slice(h * HEAD_DIM, (h + 1) * HEAD_DIM)
            a_h = acc[:, hs]
            o_ref[rs, hs] = (a_h * ta + pltpu.roll(a_h, HEAD_DIM // 2, axis=1) * tb).astype(BF16)


def _section_tables(seq):
    inv_freq = ROPE_THETA ** (-jnp.arange(0, HEAD_DIM, 2, dtype=F32) / HEAD_DIM)
    ang = jnp.arange(seq, dtype=F32)[:, None] * inv_freq[None, :]
    cos, sin = jnp.cos(ang), jnp.sin(ang)
    cos_t = jnp.concatenate([cos, cos], axis=-1)
    sin_t = jnp.concatenate([-sin, sin], axis=-1)
    one, zero = jnp.ones_like(cos_t), jnp.zeros_like(cos_t)
    ta = jnp.stack([cos_t * SCALE, cos_t, one * (SCALE * LOG2_E), one])
    tb = jnp.stack([sin_t * SCALE, sin_t, zero, zero])
    return ta, tb


def _qkv_proj(x2, gain, w_in, ta, tb, seq):
    T = x2.shape[0]
    tm = PROJ_TM
    n_pos_blocks = seq // tm

    def table_blk(i, j):
        kind = jnp.where(j == 3, 0, jnp.where(j == 4, 1, jnp.where(j == 0, 2, 3)))
        return (kind, i % n_pos_blocks, 0)

    return pl.pallas_call(
        _qkv_kernel,
        grid=(T // tm, 6),
        in_specs=[
            pl.BlockSpec((tm, D_MODEL), lambda i, j: (i, 0)),
            pl.BlockSpec((1, D_MODEL), lambda i, j: (0, 0)),
            pl.BlockSpec((D_MODEL, SECTION), lambda i, j: (0, j)),
            pl.BlockSpec((None, tm, HEAD_DIM), table_blk),
            pl.BlockSpec((None, tm, HEAD_DIM), table_blk),
        ],
        out_specs=pl.BlockSpec((tm, SECTION), lambda i, j: (i, j)),
        out_shape=jax.ShapeDtypeStruct((T, 3 * (D_SB + D_DIL)), BF16),
        scratch_shapes=[pltpu.VMEM((tm, D_MODEL), BF16)],
        compiler_params=pltpu.CompilerParams(
            dimension_semantics=("arbitrary", "arbitrary"), vmem_limit_bytes=VMEM_LIMIT),
        name="qkv_proj",
    )(x2, gain, w_in, ta, tb)


def _sb_kernel(q_ref, k_ref, v_ref, g_ref, uu_ref, o_ref, c_ref, acc_ref):
    seq = q_ref.shape[0]
    tq, tk = SB_TQ, SB_TK
    nt = (((1,), (1,)), ((), ()))
    heads = range(SB_HEADS)
    hs = [slice(h * HEAD_DIM, (h + 1) * HEAD_DIM) for h in heads]

    def aligned(row):
        return row if isinstance(row, int) else pl.multiple_of(row, tq)

    def key_tile(qb, k_start, width, ends_on_diagonal):
        n_sub = width // tk
        q0 = aligned(qb * tq)
        k0 = aligned(k_start)
        z = [lax.dot_general(q_ref[pl.ds(q0, tq), hs[h]], k_ref[pl.ds(k0, width), hs[h]], nt,
                             preferred_element_type=F32) for h in heads]
        log_beta = [jnp.minimum(z[h], 0.0) - jnp.log2(1.0 + jnp.exp2(-jnp.abs(z[h]))) for h in heads]
        log_keep = [log_beta[h] - z[h] for h in heads]
        masked = ends_on_diagonal
        if masked:
            causal = (lax.broadcasted_iota(jnp.int32, (tq, width), 0) + (width - tq)
                      > lax.broadcasted_iota(jnp.int32, (tq, width), 1))
            log_keep = [jnp.where(causal, lk, 0.0) for lk in log_keep]
        hi = [lk.astype(BF16) for lk in log_keep]
        lo = [(log_keep[h] - hi[h].astype(F32)).astype(BF16) for h in heads]
        r = [[jnp.dot(jnp.concatenate([hi[h][:, j * tk:(j + 1) * tk], lo[h][:, j * tk:(j + 1) * tk]], axis=1),
                      uu_ref[...], preferred_element_type=F32) for j in range(n_sub)] for h in heads]
        pv = []
        for h in heads:
            c = c_ref[h]
            a = [None] * n_sub
            for j in range(n_sub - 1, -1, -1):
                a[j] = jnp.exp2(log_beta[h][:, j * tk:(j + 1) * tk] + r[h][j][:, :tk] + c)
                c = c + r[h][j][:, tk:]
            c_ref[h] = c
            a = jnp.concatenate(a, axis=1)
            if masked:
                a = jnp.where(causal, a, 0.0)
            pv.append(jnp.dot(a.astype(BF16), v_ref[pl.ds(k0, width), hs[h]], preferred_element_type=F32))
        for h in heads:
            acc_ref[h] += pv[h]

    def c_max():
        return jnp.max(functools.reduce(jnp.maximum, [c_ref[h] for h in heads]))

    def finish(qb):
        q0 = aligned(qb * tq)
        for h in heads:
            o_ref[pl.ds(q0, tq), hs[h]] = _rms(acc_ref[h], g_ref[:, hs[h]]).astype(o_ref.dtype)

    def q_tile(qb, _):
        c_ref[...] = jnp.zeros_like(c_ref)
        acc_ref[...] = jnp.zeros_like(acc_ref)
        key_tile(qb, (qb - 1) * tq, 2 * tq, True)

        def live(carry):
            kt, c_hi = carry
            return (kt >= 0) & (c_hi > SB_DEAD_LOG2)

        def earlier(carry):
            kt, _ = carry
            key_tile(qb, kt * tq, tq, False)
            return kt - 1, c_max()

        lax.while_loop(live, earlier, (qb - 2, c_max()))
        finish(qb)
        return 0

    c_ref[...] = jnp.zeros_like(c_ref)
    acc_ref[...] = jnp.zeros_like(acc_ref)
    key_tile(0, 0, tq, True)
    finish(0)
    lax.fori_loop(1, seq // tq, q_tile, 0)


def _sb_attention(proj, gain, batch, seq):
    width = SB_HEADS * HEAD_DIM
    n_groups = D_SB // width
    p3 = proj.reshape(batch, seq, proj.shape[-1])
    tri = (jnp.arange(SB_TK)[:, None] > jnp.arange(SB_TK)[None, :])
    u = jnp.concatenate([tri.astype(BF16), jnp.ones((SB_TK, SB_TK), BF16)], axis=1)
    uu = jnp.concatenate([u, u], axis=0)
    out = pl.pallas_call(
        _sb_kernel,
        grid=(batch, n_groups),
        in_specs=[
            pl.BlockSpec((None, seq, width), lambda b, g: (b, 0, g)),
            pl.BlockSpec((None, seq, width), lambda b, g: (b, 0, n_groups + g)),
            pl.BlockSpec((None, seq, width), lambda b, g: (b, 0, 2 * n_groups + g)),
            pl.BlockSpec((1, width), lambda b, g: (0, g)),
            pl.BlockSpec((2 * SB_TK, 2 * SB_TK), lambda b, g: (0, 0)),
        ],
        out_specs=pl.BlockSpec((None, seq, width), lambda b, g: (b, 0, g)),
        out_shape=jax.ShapeDtypeStruct((batch, seq, D_SB), BF16),
        scratch_shapes=[pltpu.VMEM((SB_HEADS, SB_TQ, SB_TK), F32),
                        pltpu.VMEM((SB_HEADS, SB_TQ, HEAD_DIM), F32)],
        compiler_params=pltpu.CompilerParams(
            dimension_semantics=("arbitrary", "arbitrary"), vmem_limit_bytes=VMEM_LIMIT),
        name="sb_attn",
    )(p3, p3, p3, gain, uu)
    return out.reshape(batch * seq, D_SB)


def _dil_kernel(*refs):
    g_ref, o_ref, qkv_s, out_s, lse_s = refs[3 * DIL_HEADS:]
    seq = o_ref.shape[0]
    for i in range(3 * DIL_HEADS):
        qkv_s[i] = refs[i][...].astype(F32)
    q_refs, k_refs, v_refs = ([qkv_s.at[p * DIL_HEADS + h] for h in range(DIL_HEADS)] for p in range(3))
    qb = DIL_QB
    nt = (((1,), (1,)), ((), ()))
    i2 = lax.broadcasted_iota(jnp.int32, (qb, 2 * qb), 0)
    j2 = lax.broadcasted_iota(jnp.int32, (qb, 2 * qb), 1)
    band = (j2 >= i2) & (j2 <= i2 + N_BACK)
    i1 = lax.broadcasted_iota(jnp.int32, (qb, qb), 0)
    j1 = lax.broadcasted_iota(jnp.int32, (qb, qb), 1)
    tril = j1 <= i1

    def rows(start, d):
        return pl.ds(start, qb, stride=d) if d > 1 else pl.ds(pl.multiple_of(start, qb), qb)

    def block_results(i, d):
        n_blocks = seq // d // qb
        chains = [(h, i * DIL_UNROLL + u) for h in range(DIL_HEADS) for u in range(DIL_UNROLL)]
        cur, q, k, v, valid = [], [], [], [], []
        for h, g in chains:
            r = g // n_blocks
            n = g % n_blocks
            c0 = r + n * (qb * d)
            cur.append(c0)
            q.append(q_refs[h][rows(c0, d), :].astype(BF16))
            if n_blocks > 1:
                p0 = r + jnp.maximum(n - 1, 0) * (qb * d)
                k.append(jnp.concatenate([k_refs[h][rows(p0, d), :], k_refs[h][rows(c0, d), :]], axis=0))
                v.append(jnp.concatenate([v_refs[h][rows(p0, d), :], v_refs[h][rows(c0, d), :]], axis=0))
                valid.append(band & (j2 >= jnp.where(n > 0, 0, qb)))
            else:
                k.append(k_refs[h][rows(c0, d), :])
                v.append(v_refs[h][rows(c0, d), :])
                valid.append(tril)
        idx = range(len(chains))
        s = [lax.dot_general(q[c], k[c].astype(BF16), nt, preferred_element_type=F32) for c in idx]
        s = [jnp.where(valid[c], s[c], -jnp.inf) for c in idx]
        m = [jnp.max(s[c], axis=1, keepdims=True) for c in idx]
        p = [jnp.exp(s[c] - m[c]) for c in idx]
        den = [jnp.sum(p[c], axis=1, keepdims=True) for c in idx]
        acc = [jnp.dot(p[c].astype(BF16), v[c].astype(BF16), preferred_element_type=F32) for c in idx]
        return chains, cur, [acc[c] * (1.0 / den[c]) for c in idx], [m[c] + jnp.log(den[c]) for c in idx]

    strided = [d for d in DILATIONS if d > 1]
    for slot, d in enumerate(strided):
        def sub_blocks(i, _, slot=slot, d=d):
            chains, cur, outs, lses = block_results(i, d)
            for c, (h, _) in enumerate(chains):
                out_s[h * len(strided) + slot, rows(cur[c], d), :] = outs[c]
                lse_s[h * len(strided) + slot, rows(cur[c], d), :] = jnp.broadcast_to(lses[c], (qb, HEAD_DIM))
            return 0

        lax.fori_loop(0, seq // qb // DIL_UNROLL, sub_blocks, 0)

    def merge(i, _):
        chains, cur, outs, lses = block_results(i, 1)
        for c, (h, _) in enumerate(chains):
            rs = rows(cur[c], 1)
            hs = slice(h * HEAD_DIM, (h + 1) * HEAD_DIM)
            lse = [jnp.broadcast_to(lses[c], (qb, HEAD_DIM))]
            lse += [lse_s[h * len(strided) + slot, rs, :] for slot in range(len(strided))]
            branch = [outs[c]] + [out_s[h * len(strided) + slot, rs, :] for slot in range(len(strided))]
            mx = functools.reduce(jnp.maximum, lse)
            w = [jnp.exp(l - mx) for l in lse]
            num = sum(w_b * o_b for w_b, o_b in zip(w, branch))
            o_ref[rs, hs] = _rms(num / sum(w), g_ref[:, hs]).astype(o_ref.dtype)
        return 0

    lax.fori_loop(0, seq // qb // DIL_UNROLL, merge, 0)


def _dil_attention(proj, gain, batch, seq):
    width = DIL_HEADS * HEAD_DIM
    n_groups = D_DIL // width
    p3 = proj.reshape(batch, seq, proj.shape[-1])
    out = pl.pallas_call(
        _dil_kernel,
        grid=(batch, n_groups),
        in_specs=[
            pl.BlockSpec((None, seq, HEAD_DIM),
                         functools.partial(lambda b, g, *, off: (b, 0, off + g * DIL_HEADS),
                                           off=(3 * N_HEADS_SB + part * N_HEADS_DIL) + h))
            for part in range(3) for h in range(DIL_HEADS)
        ] + [pl.BlockSpec((1, width), lambda b, g: (0, g))],
        out_specs=pl.BlockSpec((None, seq, width), lambda b, g: (b, 0, g)),
        out_shape=jax.ShapeDtypeStruct((batch, seq, D_DIL), BF16),
        scratch_shapes=[pltpu.VMEM((3 * DIL_HEADS, seq, HEAD_DIM), F32)]
        + [pltpu.VMEM((DIL_HEADS * (len(DILATIONS) - 1), seq, HEAD_DIM), F32)] * 2,
        compiler_params=pltpu.CompilerParams(
            dimension_semantics=("arbitrary", "arbitrary"), vmem_limit_bytes=VMEM_LIMIT),
        name="dil_attn",
    )(*([p3] * (3 * DIL_HEADS)), gain)
    return out.reshape(batch * seq, D_DIL)


def _out_kernel(sb_ref, dl_ref, w_ref, x_ref, gpost_ref, gpre_ref, x1_ref, h2_ref):
    rows_per = sb_ref.shape[0] // OUT_ROW_GROUPS
    for r in range(OUT_ROW_GROUPS):
        rs = slice(r * rows_per, (r + 1) * rows_per)
        mix = jnp.dot(sb_ref[rs, :], w_ref[:D_SB, :], preferred_element_type=F32)
        mix += jnp.dot(dl_ref[rs, :], w_ref[D_SB:, :], preferred_element_type=F32)
        x1 = x_ref[rs, :] + _rms(mix, gpost_ref[...])
        x1_ref[rs, :] = x1
        h2_ref[rs, :] = _rms(x1, gpre_ref[...]).astype(BF16)


def _out_proj(mixed_sb, mixed_dl, w_out, x2, g_post, g_pre_ffn):
    T = x2.shape[0]
    tm = OUT_TM
    return pl.pallas_call(
        _out_kernel,
        grid=(T // tm,),
        in_specs=[
            pl.BlockSpec((tm, D_SB), lambda i: (i, 0)),
            pl.BlockSpec((tm, D_DIL), lambda i: (i, 0)),
            pl.BlockSpec((D_SB + D_DIL, D_MODEL), lambda i: (0, 0), pipeline_mode=pl.Buffered(1)),
            pl.BlockSpec((tm, D_MODEL), lambda i: (i, 0)),
            pl.BlockSpec((1, D_MODEL), lambda i: (0, 0)),
            pl.BlockSpec((1, D_MODEL), lambda i: (0, 0)),
        ],
        out_specs=[
            pl.BlockSpec((tm, D_MODEL), lambda i: (i, 0)),
            pl.BlockSpec((tm, D_MODEL), lambda i: (i, 0)),
        ],
        out_shape=[
            jax.ShapeDtypeStruct((T, D_MODEL), F32),
            jax.ShapeDtypeStruct((T, D_MODEL), BF16),
        ],
        compiler_params=pltpu.CompilerParams(
            dimension_semantics=("arbitrary",), vmem_limit_bytes=VMEM_LIMIT),
        name="out_proj",
    )(mixed_sb, mixed_dl, w_out, x2, g_post, g_pre_ffn)


def _ffn_kernel(h_ref, wg_ref, wv_ref, cwg_ref, cwv_ref, cbg_ref, cbv_ref, wdn_ref, x1_hbm, g_ref,
                o_ref, tail_ref, x1_ref, x1_sem, *, tiles_per_seq):
    i = pl.program_id(0)
    f = pl.program_id(1)
    tm, halo = FFN_TM, FFN_HALO

    @pl.when((i == 0) & (f == 0))
    def _():
        tail_ref[...] = jnp.zeros_like(tail_ref)

    def x1_copy():
        rows = pl.ds(pl.multiple_of(i * tm, tm), tm)
        return pltpu.make_async_copy(x1_hbm.at[rows, :], x1_ref, x1_sem)

    @pl.when(f == 0)
    def _():
        x1_copy().start()
        o_ref[...] = jnp.zeros_like(o_ref)

    keep = (i % tiles_per_seq) != 0

    def up_conv(part, w_ref, cw_ref, cb_ref):
        u = jnp.dot(h_ref[...], w_ref[...], preferred_element_type=F32)
        prev = jnp.where(keep, tail_ref[f, part], 0.0)
        tail_ref[f, part] = u[tm - halo:, :]
        u = jnp.concatenate([prev, u], axis=0)
        cw = cw_ref[...]
        conv = cb_ref[...] + cw[CONV_WIDTH - 1:CONV_WIDTH, :] * u[halo:, :]
        for tap in range(CONV_WIDTH - 1):
            back = CONV_WIDTH - 1 - tap
            conv += cw[tap:tap + 1, :] * u[halo - back:halo - back + tm, :]
        return conv

    gate = up_conv(0, wg_ref, cwg_ref, cbg_ref)
    value = up_conv(1, wv_ref, cwv_ref, cbv_ref)
    y = jax.nn.gelu(gate, approximate=True) * value
    o_ref[...] += jnp.dot(y.astype(BF16), wdn_ref[...], preferred_element_type=F32)

    @pl.when(f == pl.num_programs(1) - 1)
    def _():
        x1_copy().wait()
        o_ref[...] = x1_ref[...] + _rms(o_ref[...], g_ref[...])


def _conv_ffn(h2, w_up_p, conv_w_p, conv_b_p, w_down_p, x1, g_post, seq):
    T = h2.shape[0]
    tm, fc, halo = FFN_TM, FFN_FC, FFN_HALO
    n_chunks = D_FF_PAD // fc
    gate_blk = lambda i, f: (0, f)
    value_blk = lambda i, f: (0, n_chunks + f)
    return pl.pallas_call(
        functools.partial(_ffn_kernel, tiles_per_seq=seq // tm),
        grid=(T // tm, n_chunks),
        in_specs=[
            pl.BlockSpec((tm, D_MODEL), lambda i, f: (i, 0)),
            pl.BlockSpec((D_MODEL, fc), gate_blk),
            pl.BlockSpec((D_MODEL, fc), value_blk),
            pl.BlockSpec((CONV_WIDTH, fc), gate_blk),
            pl.BlockSpec((CONV_WIDTH, fc), value_blk),
            pl.BlockSpec((1, fc), gate_blk),
            pl.BlockSpec((1, fc), value_blk),
            pl.BlockSpec((fc, D_MODEL), lambda i, f: (f, 0)),
            pl.BlockSpec(memory_space=pl.ANY),
            pl.BlockSpec((1, D_MODEL), lambda i, f: (0, 0)),
        ],
        out_specs=pl.BlockSpec((tm, D_MODEL), lambda i, f: (i, 0)),
        out_shape=jax.ShapeDtypeStruct((T, D_MODEL), F32),
        scratch_shapes=[pltpu.VMEM((n_chunks, 2, halo, fc), F32),
                        pltpu.VMEM((tm, D_MODEL), F32),
                        pltpu.SemaphoreType.DMA(())],
        compiler_params=pltpu.CompilerParams(
            dimension_semantics=("arbitrary", "arbitrary"), vmem_limit_bytes=FFN_VMEM_LIMIT),
        name="conv_ffn",
    )(h2, w_up_p, w_up_p, conv_w_p, conv_w_p, conv_b_p, conv_b_p, w_down_p, x1, g_post)


def _pad_gate_value(a, dtype):
    zeros = jnp.zeros((a.shape[0], D_FF_PAD - D_FF), dtype)
    return jnp.concatenate([a[:, :D_FF].astype(dtype), zeros, a[:, D_FF:].astype(dtype), zeros], axis=1)


def _stage_up_kernel(w_ref, o_ref):
    rows = w_ref.shape[0]
    for part in range(2):
        o_ref[:, part * D_FF_PAD:part * D_FF_PAD + D_FF] = w_ref[:, part * D_FF:(part + 1) * D_FF].astype(BF16)
        o_ref[:, part * D_FF_PAD + D_FF:(part + 1) * D_FF_PAD] = jnp.zeros((rows, D_FF_PAD - D_FF), BF16)


def _stage_down_kernel(w_ref, o_ref):
    o_ref[:D_FF, :] = w_ref[...].astype(BF16)
    o_ref[D_FF:, :] = jnp.zeros((D_FF_PAD - D_FF, w_ref.shape[1]), BF16)


def _stage_ffn_weights(w_up, w_down):
    rows, cols = STAGE_UP_ROWS, STAGE_DOWN_COLS
    params = pltpu.CompilerParams(dimension_semantics=("arbitrary",), vmem_limit_bytes=VMEM_LIMIT)
    w_up_p = pl.pallas_call(
        _stage_up_kernel,
        grid=(D_MODEL // rows,),
        in_specs=[pl.BlockSpec((rows, 2 * D_FF), lambda i: (i, 0))],
        out_specs=pl.BlockSpec((rows, 2 * D_FF_PAD), lambda i: (i, 0)),
        out_shape=jax.ShapeDtypeStruct((D_MODEL, 2 * D_FF_PAD), BF16),
        compiler_params=params,
        name="stage_w_up",
    )(w_up)
    w_down_p = pl.pallas_call(
        _stage_down_kernel,
        grid=(D_MODEL // cols,),
        in_specs=[pl.BlockSpec((D_FF, cols), lambda i: (0, i))],
        out_specs=pl.BlockSpec((D_FF_PAD, cols), lambda i: (0, i)),
        out_shape=jax.ShapeDtypeStruct((D_FF_PAD, D_MODEL), BF16),
        compiler_params=params,
        name="stage_w_down",
    )(w_down)
    return w_up_p, w_down_p


def kernel(x, pre_mix_gain, post_mix_gain, pre_ffn_gain, post_ffn_gain, w_in, sb_out_gain, dil_out_gain,
           w_out, w_up, conv_w, conv_b, w_down):
    batch, seq, _ = x.shape
    depth = w_in.shape[0]
    ta, tb = _section_tables(seq)
    x2 = x.reshape(batch * seq, D_MODEL)
    for layer in range(depth):
        row = lambda g: g[layer][None, :]
        proj = _qkv_proj(x2, row(pre_mix_gain), w_in[layer].astype(BF16), ta, tb, seq)
        mixed_sb = _sb_attention(proj, row(sb_out_gain), batch, seq)
        mixed_dl = _dil_attention(proj, row(dil_out_gain), batch, seq)
        x1, h2 = _out_proj(mixed_sb, mixed_dl, w_out[layer].astype(BF16), x2,
                           row(post_mix_gain), row(pre_ffn_gain))
        w_up_p, w_down_p = _stage_ffn_weights(w_up[layer], w_down[layer])
        x2 = _conv_ffn(h2, w_up_p, _pad_gate_value(conv_w[layer], F32), _pad_gate_value(row(conv_b), F32),
                       w_down_p, x1, row(post_ffn_gain), seq)
    return x2.reshape(batch, seq, D_MODEL)
```
